```python
import math
import jax, jax.numpy as jnp
from jax import lax
import numpy as np

D_MODEL = 1024
BATCH = 8
SEQ = 8192
DEPTH = 2

CHUNK = 64

N_A = DEPTH // 2
N_B = DEPTH - N_A

POOL_WINDOWS = (2, 4, 8, 16)
N_POOL_GROUPS = len(POOL_WINDOWS)
POOL_GROUP = D_MODEL // N_POOL_GROUPS

N_HEADS = 16
HEAD_DIM = D_MODEL // N_HEADS
Q_BLOCK = 128
NEG_INF = -1e30

N_EXPERTS = 32
TOP_K = 4
D_FF = D_MODEL
SWIGLU_LIMIT = 7.0
SWIGLU_ALPHA = 1.702

PLE_DIM = 256

RMS_EPS = 1e-6

kernel_name = "yoco_pool_fox_moe_ple"


def rms_norm(x, g):
    xf = x.astype(jnp.float32)
    y = xf * lax.rsqrt(jnp.mean(xf * xf, axis=-1, keepdims=True) + RMS_EPS)
    return (y * g.astype(jnp.float32)).astype(x.dtype)


def pool_mixer(h, w_groups, scale):
    s = h.shape[1]
    pos = jnp.arange(s)
    outs = []
    for g, w in enumerate(POOL_WINDOWS):
        hg = h[..., g * POOL_GROUP:(g + 1) * POOL_GROUP]
        hf = hg.astype(jnp.float32)
        cs = jnp.cumsum(hf, axis=1)
        lag = jnp.pad(cs, ((0, 0), (w, 0), (0, 0)))[:, :s]
        count = jnp.minimum(pos + 1, w).astype(jnp.float32)[None, :, None]
        diff = ((cs - lag) / count - hf).astype(h.dtype)
        outs.append(jnp.einsum('bsc,cd->bsd', diff, w_groups[g]))
    return jnp.concatenate(outs, axis=-1) * scale


def shared_key_side(x, kv_norm, w_k, w_v, w_fgate, b_fgate):
    b, s, _ = x.shape
    hk = rms_norm(x, kv_norm)
    k = (hk @ w_k).reshape(b, s, N_HEADS, HEAD_DIM).transpose(0, 2, 1, 3)
    v = (hk @ w_v).reshape(b, s, N_HEADS, HEAD_DIM).transpose(0, 2, 1, 3)
    log_f = jax.nn.log_sigmoid((hk @ w_fgate + b_fgate).astype(jnp.float32))
    dc = jnp.cumsum(log_f, axis=1).transpose(0, 2, 1)
    return k, v, dc


def forgetting_attention(h, k, v, dc, w_q, w_o):
    b, s, _ = h.shape
    nb = s // Q_BLOCK
    q = (h @ w_q).reshape(b, s, N_HEADS, HEAD_DIM).transpose(0, 2, 1, 3)
    q_blocks = q.reshape(b, N_HEADS, nb, Q_BLOCK, HEAD_DIM).transpose(2, 0, 1, 3, 4)
    dc_blocks = dc.reshape(b, N_HEADS, nb, Q_BLOCK).transpose(2, 0, 1, 3)
    q_pos = jnp.arange(s).reshape(nb, Q_BLOCK)
    k_pos = jnp.arange(s)
    scale = 1.0 / math.sqrt(HEAD_DIM)

    def one_block(args):
        qi, dqi, pi = args
        logits = jnp.einsum('bhqd,bhkd->bhqk', qi, k).astype(jnp.float32) * scale
        logits = logits + dqi[..., :, None] - dc[:, :, None, :]
        logits = jnp.where(k_pos[None, :] <= pi[:, None], logits, NEG_INF)
        probs = jax.nn.softmax(logits, axis=-1).astype(v.dtype)
        return jnp.einsum('bhqk,bhkd->bhqd', probs, v)

    o = lax.map(one_block, (q_blocks, dc_blocks, q_pos))
    o = o.transpose(1, 0, 3, 2, 4).reshape(b, s, N_HEADS * HEAD_DIM)
    return o @ w_o


def moe_ffn(h, router_w, router_b, w_gu, b_gu, w_down, b_down):
    b, s, d = h.shape
    t = h.reshape(b * s, d)
    logits = (t @ router_w + router_b).astype(jnp.float32)
    top_v, top_i = lax.top_k(logits, TOP_K)
    gates = jax.nn.softmax(top_v, axis=-1)
    dense_gates = jnp.einsum('nk,nke->ne', gates,
                             jax.nn.one_hot(top_i, N_EXPERTS, dtype=jnp.float32)).astype(h.dtype)
    out = jnp.zeros_like(t)
    for e in range(N_EXPERTS):
        gu = t @ w_gu[e] + b_gu[e]
        gate = jnp.minimum(gu[:, :D_FF], SWIGLU_LIMIT)
        up = jnp.clip(gu[:, D_FF:], -SWIGLU_LIMIT, SWIGLU_LIMIT)
        y = (up + 1.0) * (gate * jax.nn.sigmoid(SWIGLU_ALPHA * gate))
        out = out + dense_gates[:, e:e + 1] * (y @ w_down[e] + b_down[e])
    return out.reshape(b, s, d)


def per_layer_embed(x, p_i, norm_g, w_gate, b_gate, w_proj):
    gate = jax.nn.sigmoid(rms_norm(x, norm_g) @ w_gate + b_gate)
    return gate * (p_i @ w_proj)


def setup_inputs(seed: int = 0) -> dict:
    key = jax.random.key(seed)
    ks = jax.random.split(key, 26)
    f32 = jnp.float32

    def nrm(k, shape, scale):
        return jax.random.normal(k, shape, f32) * scale

    hd = N_HEADS * HEAD_DIM
    return {
        "x": nrm(ks[0], (BATCH, SEQ, D_MODEL), 1.0),
        "p": nrm(ks[1], (DEPTH, BATCH, SEQ, PLE_DIM), 1.0),
        "norm_mix": 1.0 + nrm(ks[2], (DEPTH, D_MODEL), 0.1),
        "norm_ffn": 1.0 + nrm(ks[3], (DEPTH, D_MODEL), 0.1),
        "norm_ple": 1.0 + nrm(ks[4], (DEPTH, D_MODEL), 0.1),
        "norm_final": 1.0 + nrm(ks[5], (D_MODEL,), 0.1),
        "pool_w": nrm(ks[6], (N_A, N_POOL_GROUPS, POOL_GROUP, POOL_GROUP), POOL_GROUP ** -0.5),
        "pool_scale": 0.5 + nrm(ks[7], (N_A, D_MODEL), 0.1),
        "kv_norm": 1.0 + nrm(ks[8], (D_MODEL,), 0.1),
        "w_k": nrm(ks[9], (D_MODEL, hd), D_MODEL ** -0.5),
        "w_v": nrm(ks[10], (D_MODEL, hd), D_MODEL ** -0.5),
        "w_fgate": nrm(ks[11], (D_MODEL, N_HEADS), D_MODEL ** -0.5),
        "b_fgate": 3.0 + nrm(ks[12], (N_HEADS,), 0.1),
        "w_q": nrm(ks[13], (N_B, D_MODEL, hd), D_MODEL ** -0.5),
        "w_o": nrm(ks[14], (N_B, hd, D_MODEL), hd ** -0.5),
        "router_w": nrm(ks[15], (DEPTH, D_MODEL, N_EXPERTS), D_MODEL ** -0.5),
        "router_b": nrm(ks[16], (DEPTH, N_EXPERTS), 0.01),
        "exp_w_gu": nrm(ks[17], (DEPTH, N_EXPERTS, D_MODEL, 2 * D_FF), D_MODEL ** -0.5),
        "exp_b_gu": nrm(ks[18], (DEPTH, N_EXPERTS, 2 * D_FF), 0.02),
        "exp_w_down": nrm(ks[19], (DEPTH, N_EXPERTS, D_FF, D_MODEL), D_FF ** -0.5),
        "exp_b_down": nrm(ks[20], (DEPTH, N_EXPERTS, D_MODEL), 0.02),
        "ple_w_gate": nrm(ks[21], (DEPTH, D_MODEL, D_MODEL), D_MODEL ** -0.5),
        "ple_b_gate": nrm(ks[22], (DEPTH, D_MODEL), 0.02),
        "ple_w_proj": nrm(ks[23], (DEPTH, PLE_DIM, D_MODEL), PLE_DIM ** -0.5),
    }


def reference(x, p, norm_mix, norm_ffn, norm_ple, norm_final, pool_w, pool_scale,
              kv_norm, w_k, w_v, w_fgate, b_fgate, w_q, w_o,
              router_w, router_b, exp_w_gu, exp_b_gu, exp_w_down, exp_b_down,
              ple_w_gate, ple_b_gate, ple_w_proj):
    k_sh = v_sh = dc_sh = None
    for i in range(DEPTH):
        h = rms_norm(x, norm_mix[i])
        if i < N_A:
            x = x + pool_mixer(h, pool_w[i], pool_scale[i])
        else:
            j = i - N_A
            x = x + forgetting_attention(h, k_sh, v_sh, dc_sh, w_q[j], w_o[j])
        x = x + moe_ffn(rms_norm(x, norm_ffn[i]), router_w[i], router_b[i],
                        exp_w_gu[i], exp_b_gu[i], exp_w_down[i], exp_b_down[i])
        x = x + per_layer_embed(x, p[i], norm_ple[i], ple_w_gate[i], ple_b_gate[i], ple_w_proj[i])
        if i == N_A - 1:
            k_sh, v_sh, dc_sh = shared_key_side(x, kv_norm, w_k, w_v, w_fgate, b_fgate)
    return rms_norm(x, norm_final)
```

```python
import functools
import math

import numpy as np
import jax
import jax.numpy as jnp
from jax import lax
from jax.experimental import pallas as pl
from jax.experimental.pallas import tpu as pltpu

N_HEADS = 16
TOP_K = 4
POOL_WINDOWS = (2, 4, 8, 16)
POOL_HALO = 16
SWIGLU_LIMIT = 7.0
SWIGLU_ALPHA = 1.702
RMS_EPS = 1e-6
NEG_INF = -1e30

F32 = jnp.float32
BF16 = jnp.bfloat16
HIGHEST = lax.Precision.HIGHEST

V7X_VMEM_BYTES = 64 * 1024 * 1024
VMEM_LIMIT = V7X_VMEM_BYTES - 8 * 1024 * 1024


def _cparams(n_axes):
    return pltpu.CompilerParams(
        dimension_semantics=("arbitrary",) * n_axes,
        vmem_limit_bytes=VMEM_LIMIT,
    )


def _rms(v, g):
    return v * lax.rsqrt(jnp.mean(v * v, axis=-1, keepdims=True) + RMS_EPS) * g


def _route(h2, rw_ref, rb_ref, cnt_ref, idx_ref, gate_ref, rank_ref):
    t = h2.shape[0]
    e = rw_ref.shape[1]
    logits = jnp.dot(h2, rw_ref[...], precision=HIGHEST, preferred_element_type=F32) + rb_ref[...]
    lane = lax.broadcasted_iota(jnp.int32, (t, e), 1)
    work = logits
    sels, vals, idxs = [], [], []
    for _ in range(TOP_K):
        m = jnp.max(work, axis=-1, keepdims=True)
        idx = jnp.min(jnp.where(work == m, lane, e), axis=-1, keepdims=True)
        sel = lane == idx
        work = jnp.where(sel, -jnp.inf, work)
        sels.append(sel)
        vals.append(m)
        idxs.append(idx)
    exps = [jnp.exp(v - vals[0]) for v in vals]
    denom = exps[0] + exps[1] + exps[2] + exps[3]
    onehot = sels[0].astype(F32) + sels[1].astype(F32) + sels[2].astype(F32) + sels[3].astype(F32)
    row = lax.broadcasted_iota(jnp.int32, (t, t), 0)
    col = lax.broadcasted_iota(jnp.int32, (t, t), 1)
    strict_lower = (col < row).astype(BF16)
    excl = jnp.dot(strict_lower, onehot.astype(BF16), preferred_element_type=F32)
    base = cnt_ref[...]
    tot = excl + base
    lane_k = lax.broadcasted_iota(jnp.int32, (t, TOP_K), 1)
    idx_out = jnp.zeros((t, TOP_K), jnp.int32)
    gate_out = jnp.zeros((t, TOP_K), F32)
    rank_out = jnp.zeros((t, TOP_K), jnp.int32)
    for k in range(TOP_K):
        rank_k = jnp.sum(jnp.where(sels[k], tot, 0.0), axis=-1, keepdims=True)
        idx_out = jnp.where(lane_k == k, idxs[k], idx_out)
        gate_out = jnp.where(lane_k == k, exps[k] / denom, gate_out)
        rank_out = jnp.where(lane_k == k, rank_k.astype(jnp.int32), rank_out)
    idx_ref[...] = idx_out
    gate_ref[...] = gate_out
    rank_ref[...] = rank_out
    cnt_ref[...] = base + jnp.sum(onehot, axis=0, keepdims=True)


def _mix0_kernel(x_ref, xh_ref, gmix_ref, pw_ref, ps_ref, gffn_ref, rw_ref, rb_ref,
                 x1_ref, h2_ref, idx_ref, gate_ref, rank_ref, cnt_out_ref, cnt_ref):
    b = pl.program_id(0)
    i = pl.program_id(1)
    ts = x_ref.shape[1]
    d = x_ref.shape[2]
    cg = d // len(POOL_WINDOWS)

    @pl.when((b == 0) & (i == 0))
    def _():
        cnt_ref[...] = jnp.zeros_like(cnt_ref)

    xt = x_ref[0]
    g = gmix_ref[...]
    h = _rms(xt, g)
    hh = jnp.where(i > 0, _rms(xh_ref[0], g), 0.0)
    hcat = jnp.concatenate([hh, h], axis=0)
    pos = i * ts + lax.broadcasted_iota(jnp.int32, (ts, 1), 0)
    outs = []
    for gi, w in enumerate(POOL_WINDOWS):
        cur = hcat[:, gi * cg:(gi + 1) * cg]
        span = 1
        while span < w:
            cur = cur + pltpu.roll(cur, span, 0)
            span *= 2
        wsum = cur[POOL_HALO:, :]
        count = jnp.minimum(pos + 1, w).astype(F32)
        diff = wsum / count - h[:, gi * cg:(gi + 1) * cg]
        outs.append(jnp.dot(diff.astype(BF16), pw_ref[gi], preferred_element_type=F32))
    mixed = jnp.concatenate(outs, axis=-1) * ps_ref[...]
    x1 = xt + mixed
    x1_ref[0] = x1
    h2 = _rms(x1, gffn_ref[...])
    h2_ref[0] = h2
    _route(h2, rw_ref, rb_ref, cnt_ref, idx_ref, gate_ref, rank_ref)
    cnt_out_ref[...] = cnt_ref[...]


def _mix0(x, g_mix, pool_w, pool_scale, g_ffn, router_w, router_b, ts):
    b, s, d = x.shape
    e = router_w.shape[1]
    n = b * s
    nt = s // ts
    hb = ts // POOL_HALO
    ng = len(POOL_WINDOWS)
    cg = d // ng
    const2 = lambda bi, i: (0, 0)
    tok = lambda bi, i: (bi * nt + i, 0)
    return pl.pallas_call(
        _mix0_kernel,
        grid=(b, nt),
        in_specs=[
            pl.BlockSpec((1, ts, d), lambda bi, i: (bi, i, 0)),
            pl.BlockSpec((1, POOL_HALO, d), lambda bi, i: (bi, jnp.maximum(i * hb - 1, 0), 0)),
            pl.BlockSpec((1, d), const2),
            pl.BlockSpec((ng, cg, cg), lambda bi, i: (0, 0, 0)),
            pl.BlockSpec((1, d), const2),
            pl.BlockSpec((1, d), const2),
            pl.BlockSpec((d, e), const2),
            pl.BlockSpec((1, e), const2),
        ],
        out_specs=[
            pl.BlockSpec((1, ts, d), lambda bi, i: (bi, i, 0)),
            pl.BlockSpec((1, ts, d), lambda bi, i: (bi, i, 0)),
            pl.BlockSpec((ts, TOP_K), tok),
            pl.BlockSpec((ts, TOP_K), tok),
            pl.BlockSpec((ts, TOP_K), tok),
            pl.BlockSpec((1, e), const2),
        ],
        out_shape=[
            jax.ShapeDtypeStruct((b, s, d), F32),
            jax.ShapeDtypeStruct((b, s, d), F32),
            jax.ShapeDtypeStruct((n, TOP_K), jnp.int32),
            jax.ShapeDtypeStruct((n, TOP_K), F32),
            jax.ShapeDtypeStruct((n, TOP_K), jnp.int32),
            jax.ShapeDtypeStruct((1, e), F32),
        ],
        scratch_shapes=[pltpu.VMEM((1, e), F32)],
        compiler_params=_cparams(2),
        name="mix0",
    )(x, x, g_mix.reshape(1, d), pool_w.astype(BF16), pool_scale.reshape(1, d),
      g_ffn.reshape(1, d), router_w, router_b.reshape(1, e))


def _plan(idx, rank, counts, tm, num_tiles):
    cnt = counts.reshape(-1).astype(jnp.int32)
    tiles_per = (cnt + tm - 1) // tm
    tile_end = jnp.cumsum(tiles_per)
    tile_start = tile_end - tiles_per
    pos = tile_start[idx] * tm + rank
    t = jnp.arange(num_tiles, dtype=jnp.int32)
    tile_expert = jnp.sum((t[:, None] >= tile_end[None, :]).astype(jnp.int32), axis=1)
    tile_expert = jnp.minimum(tile_expert, cnt.shape[0] - 1)
    n_used = tile_end[-1:].astype(jnp.int32)
    return pos.reshape(-1).astype(jnp.int32), tile_expert.astype(jnp.int32), n_used


def _fetch_idx(pos_hbm, idx_smem, sem, step, slot, count):
    start = pl.multiple_of(step * count, count)
    return pltpu.make_async_copy(pos_hbm.at[pl.ds(start, count)], idx_smem.at[slot], sem.at[slot])


def _dispatch_kernel(pos_hbm, h_ref, xs_ref, idx_smem, isem, dsem):
    s = pl.program_id(0)
    ns = pl.num_programs(0)
    td = h_ref.shape[0]
    cnt = td * TOP_K
    slot = s % 2

    @pl.when(s == 0)
    def _():
        _fetch_idx(pos_hbm, idx_smem, isem, s, slot, cnt).start()

    @pl.when(s + 1 < ns)
    def _():
        _fetch_idx(pos_hbm, idx_smem, isem, s + 1, 1 - slot, cnt).start()

    _fetch_idx(pos_hbm, idx_smem, isem, s, slot, cnt).wait()

    def row_copy(r, p):
        return pltpu.make_async_copy(h_ref.at[pl.ds(r, 1), :], xs_ref.at[pl.ds(p, 1), :], dsem)

    def body(r, c):
        for k in range(TOP_K):
            row_copy(r, idx_smem[slot, r * TOP_K + k]).start()
        return c

    lax.fori_loop(0, td, body, 0, unroll=8)
    for k in range(TOP_K):
        pltpu.make_async_copy(h_ref, xs_ref.at[pl.ds(0, td), :], dsem).wait()


def _dispatch(h2, pos, rows_padded, td):
    n, d = h2.shape
    return pl.pallas_call(
        _dispatch_kernel,
        grid=(n // td,),
        in_specs=[
            pl.BlockSpec(memory_space=pl.ANY),
            pl.BlockSpec((td, d), lambda s: (s, 0)),
        ],
        out_specs=pl.BlockSpec(memory_space=pl.ANY),
        out_shape=jax.ShapeDtypeStruct((rows_padded, d), F32),
        scratch_shapes=[
            pltpu.SMEM((2, td * TOP_K), jnp.int32),
            pltpu.SemaphoreType.DMA((2,)),
            pltpu.SemaphoreType.DMA(()),
        ],
        compiler_params=_cparams(1),
        name="dispatch",
    )(pos, h2)


def _experts_kernel(texp_ref, nused_ref, xs_ref, wgu_ref, bgu_ref, wd_ref, bd_ref, ys_ref):
    t = pl.program_id(0)
    f = wd_ref.shape[1]

    @pl.when(t < nused_ref[0])
    def _():
        xb = xs_ref[...].astype(BF16)
        gu = jnp.dot(xb, wgu_ref[0], preferred_element_type=F32) + bgu_ref[0]
        gate = jnp.minimum(gu[:, :f], SWIGLU_LIMIT)
        up = jnp.clip(gu[:, f:], -SWIGLU_LIMIT, SWIGLU_LIMIT)
        y = (up + 1.0) * (gate * jax.nn.sigmoid(SWIGLU_ALPHA * gate))
        ys_ref[...] = jnp.dot(y.astype(BF16), wd_ref[0], preferred_element_type=F32) + bd_ref[0]


def _experts(xs, tile_expert, n_used, w_gu, b_gu, w_down, b_down, tm):
    rows, d = xs.shape
    e, _, f2 = w_gu.shape
    f = f2 // 2
    num_tiles = rows // tm
    row_tile = lambda t, te, nu: (jnp.minimum(t, nu[0] - 1), 0)
    expert = lambda t, te, nu: (te[t], 0, 0)
    return pl.pallas_call(
        _experts_kernel,
        grid_spec=pltpu.PrefetchScalarGridSpec(
            num_scalar_prefetch=2,
            grid=(num_tiles,),
            in_specs=[
                pl.BlockSpec((tm, d), row_tile),
                pl.BlockSpec((1, d, f2), expert),
                pl.BlockSpec((1, 1, f2), expert),
                pl.BlockSpec((1, f, d), expert),
                pl.BlockSpec((1, 1, d), expert),
            ],
            out_specs=pl.BlockSpec((tm, d), row_tile),
        ),
        out_shape=jax.ShapeDtypeStruct((rows, d), F32),
        compiler_params=_cparams(1),
        name="experts",
    )(tile_expert, n_used, xs, w_gu, b_gu.reshape(e, 1, f2), w_down, b_down.reshape(e, 1, d))


def _combine_kernel(*refs, final):
    if final:
        (pos_hbm, ys_hbm, x_ref, gates_ref, p_ref, gple_ref, wg_ref, bg_ref, wp_ref, gfin_ref,
         out_ref, idx_smem, gbuf, isem, gsem) = refs
    else:
        (pos_hbm, ys_hbm, x_ref, gates_ref, p_ref, gple_ref, wg_ref, bg_ref, wp_ref,
         gkv_ref, wkt_ref, wv_ref, wf_ref, wft_ref, bf_ref, bft_ref, gq_ref, wq_ref,
         x3_ref, q_ref, kt_ref, v_ref, dcc_ref, dcr_ref,
         idx_smem, gbuf, isem, gsem, carry_c, carry_r) = refs
    bi = pl.program_id(0)
    i = pl.program_id(1)
    nt = pl.num_programs(1)
    step = bi * nt + i
    nsteps = pl.num_programs(0) * nt
    tc = x_ref.shape[1]
    cnt = tc * TOP_K
    slot = step % 2

    @pl.when(step == 0)
    def _():
        _fetch_idx(pos_hbm, idx_smem, isem, step, slot, cnt).start()

    @pl.when(step + 1 < nsteps)
    def _():
        _fetch_idx(pos_hbm, idx_smem, isem, step + 1, 1 - slot, cnt).start()

    _fetch_idx(pos_hbm, idx_smem, isem, step, slot, cnt).wait()

    def body(r, c):
        for k in range(TOP_K):
            p = idx_smem[slot, r * TOP_K + k]
            pltpu.make_async_copy(ys_hbm.at[pl.ds(p, 1), :], gbuf.at[k, pl.ds(r, 1), :], gsem).start()
        return c

    lax.fori_loop(0, tc, body, 0, unroll=8)

    proj = jnp.dot(p_ref[0].astype(BF16), wp_ref[...], preferred_element_type=F32)
    x1 = x_ref[0]
    gates = gates_ref[...]

    for k in range(TOP_K):
        pltpu.make_async_copy(ys_hbm.at[pl.ds(0, tc), :], gbuf.at[k], gsem).wait()

    moe = gates[:, 0:1] * gbuf[0]
    for k in range(1, TOP_K):
        moe = moe + gates[:, k:k + 1] * gbuf[k]
    x2 = x1 + moe
    hn = _rms(x2, gple_ref[...])
    pg = jax.nn.sigmoid(jnp.dot(hn.astype(BF16), wg_ref[...], preferred_element_type=F32) + bg_ref[...])
    x3 = x2 + pg * proj

    if final:
        out_ref[0] = _rms(x3, gfin_ref[...])
        return

    x3_ref[0] = x3
    hk = _rms(x3, gkv_ref[...])
    hkb = hk.astype(BF16)
    nt_dims = (((1,), (1,)), ((), ()))
    kt_ref[0] = lax.dot_general(wkt_ref[...], hkb, nt_dims, preferred_element_type=F32).astype(BF16)
    v_ref[0] = jnp.dot(hkb, wv_ref[...], preferred_element_type=F32).astype(BF16)
    zc = jnp.dot(hk, wf_ref[...], precision=HIGHEST, preferred_element_type=F32) + bf_ref[...]
    zr = lax.dot_general(wft_ref[...], hk, nt_dims, precision=HIGHEST,
                         preferred_element_type=F32) + bft_ref[...]
    lfc = jax.nn.log_sigmoid(zc)
    lfr = jax.nn.log_sigmoid(zr)

    @pl.when(i == 0)
    def _():
        carry_c[...] = jnp.zeros_like(carry_c)
        carry_r[...] = jnp.zeros_like(carry_r)

    row = lax.broadcasted_iota(jnp.int32, (tc, tc), 0)
    col = lax.broadcasted_iota(jnp.int32, (tc, tc), 1)
    lower_incl = (col <= row).astype(F32)
    upper_incl = (row <= col).astype(F32)
    dcc = jnp.dot(lower_incl, lfc, precision=HIGHEST, preferred_element_type=F32) + carry_c[...]
    dcr = jnp.dot(lfr, upper_incl, precision=HIGHEST, preferred_element_type=F32) + carry_r[...]
    dcc_ref[0] = dcc
    dcr_ref[0] = dcr
    carry_c[...] = dcc[tc - 1:tc, :]
    carry_r[...] = dcr[:, tc - 1:tc]

    hq = _rms(x3, gq_ref[...])
    q = jnp.dot(hq.astype(BF16), wq_ref[...], preferred_element_type=F32)
    q_ref[0] = q.astype(BF16)


def _combine(pos, ys, x1, gates, p_i, g_ple, w_gate, b_gate, w_proj, tc, *,
             g_final=None, kv=None):
    b, s, d = x1.shape
    pd = p_i.shape[-1]
    nt = s // tc
    final = g_final is not None
    const2 = lambda bi, i: (0, 0)
    tile3 = lambda bi, i: (bi, i, 0)
    in_specs = [
        pl.BlockSpec(memory_space=pl.ANY),
        pl.BlockSpec(memory_space=pl.ANY),
        pl.BlockSpec((1, tc, d), tile3),
        pl.BlockSpec((tc, TOP_K), lambda bi, i: (bi * nt + i, 0)),
        pl.BlockSpec((1, tc, pd), tile3),
        pl.BlockSpec((1, d), const2),
        pl.BlockSpec((d, d), const2),
        pl.BlockSpec((1, d), const2),
        pl.BlockSpec((pd, d), const2),
    ]
    args = [pos, ys, x1, gates, p_i, g_ple.reshape(1, d), w_gate.astype(BF16),
            b_gate.reshape(1, d), w_proj.astype(BF16)]
    scratch = [
        pltpu.SMEM((2, tc * TOP_K), jnp.int32),
        pltpu.VMEM((TOP_K, tc, d), F32),
        pltpu.SemaphoreType.DMA((2,)),
        pltpu.SemaphoreType.DMA(()),
    ]
    if final:
        in_specs.append(pl.BlockSpec((1, d), const2))
        args.append(g_final.reshape(1, d))
        out_specs = pl.BlockSpec((1, tc, d), tile3)
        out_shape = jax.ShapeDtypeStruct((b, s, d), F32)
    else:
        g_kv, w_k, w_v, w_f, b_f, g_q, w_q = kv
        h = w_f.shape[1]
        in_specs += [
            pl.BlockSpec((1, d), const2),
            pl.BlockSpec((d, d), const2),
            pl.BlockSpec((d, d), const2),
            pl.BlockSpec((d, h), const2),
            pl.BlockSpec((h, d), const2),
            pl.BlockSpec((1, h), const2),
            pl.BlockSpec((h, 1), const2),
            pl.BlockSpec((1, d), const2),
            pl.BlockSpec((d, d), const2),
        ]
        scale = 1.0 / math.sqrt(d // N_HEADS)
        args += [g_kv.reshape(1, d), w_k.T.astype(BF16), w_v.astype(BF16), w_f, w_f.T,
                 b_f.reshape(1, h), b_f.reshape(h, 1), g_q.reshape(1, d),
                 (w_q * scale).astype(BF16)]
        out_specs = [
            pl.BlockSpec((1, tc, d), tile3),
            pl.BlockSpec((1, tc, d), tile3),
            pl.BlockSpec((1, d, tc), lambda bi, i: (bi, 0, i)),
            pl.BlockSpec((1, tc, d), tile3),
            pl.BlockSpec((1, tc, h), tile3),
            pl.BlockSpec((1, h, tc), lambda bi, i: (bi, 0, i)),
        ]
        out_shape = [
            jax.ShapeDtypeStruct((b, s, d), F32),
            jax.ShapeDtypeStruct((b, s, d), BF16),
            jax.ShapeDtypeStruct((b, d, s), BF16),
            jax.ShapeDtypeStruct((b, s, d), BF16),
            jax.ShapeDtypeStruct((b, s, h), F32),
            jax.ShapeDtypeStruct((b, h, s), F32),
        ]
        scratch += [pltpu.VMEM((1, h), F32), pltpu.VMEM((h, 1), F32)]
    return pl.pallas_call(
        functools.partial(_combine_kernel, final=final),
        grid=(b, nt),
        in_specs=in_specs,
        out_specs=out_specs,
        out_shape=out_shape,
        scratch_shapes=scratch,
        compiler_params=_cparams(2),
        name="combine_final" if final else "combine_kv",
    )(*args)


def _attn_kernel(qi_ref, kj_ref, q_ref, kt_ref, v_ref, dcc_ref, dcr_ref, x_ref, wo_ref,
                 gffn_ref, rw_ref, rb_ref,
                 x4_ref, h2_ref, idx_ref, gate_ref, rank_ref, cnt_out_ref,
                 m_ref, l_ref, acc_ref, cnt_ref):
    b = pl.program_id(0)
    pr = pl.program_id(1)
    qi = qi_ref[pr]
    kj = kj_ref[pr]
    tq = q_ref.shape[1]
    tk = v_ref.shape[1]
    d = q_ref.shape[2]
    dh = d // N_HEADS
    pw = 2 * dh

    @pl.when((b == 0) & (pr == 0))
    def _():
        cnt_ref[...] = jnp.zeros_like(cnt_ref)

    @pl.when(kj == 0)
    def _():
        m_ref[...] = jnp.full_like(m_ref, NEG_INF)
        l_ref[...] = jnp.zeros_like(l_ref)
        acc_ref[...] = jnp.zeros_like(acc_ref)

    qpos = qi * tq + lax.broadcasted_iota(jnp.int32, (tq, tk), 0)
    kpos = kj * tk + lax.broadcasted_iota(jnp.int32, (tq, tk), 1)
    causal = kpos <= qpos
    lane = lax.broadcasted_iota(jnp.int32, (tq, pw), 1)
    first = lane < dh
    dcc = dcc_ref[0]
    dcr = dcr_ref[0]

    for hp in range(N_HEADS // 2):
        qp = q_ref[0, :, hp * pw:(hp + 1) * pw]
        ktp = kt_ref[0, hp * pw:(hp + 1) * pw, :]
        vp = v_ref[0, :, hp * pw:(hp + 1) * pw]
        pv, alpha = [], []
        for sub in range(2):
            hd = 2 * hp + sub
            qm = jnp.where(first if sub == 0 else jnp.logical_not(first), qp, jnp.zeros_like(qp))
            sc = jnp.dot(qm, ktp, preferred_element_type=F32)
            sc = sc + dcc[:, hd:hd + 1] - dcr[hd:hd + 1, :]
            sc = jnp.where(causal, sc, NEG_INF)
            m_old = m_ref[hd]
            m_new = jnp.maximum(m_old, jnp.max(sc, axis=-1, keepdims=True))
            a = jnp.exp(m_old - m_new)
            pmat = jnp.exp(sc - m_new)
            l_ref[hd] = a * l_ref[hd] + jnp.sum(pmat, axis=-1, keepdims=True)
            m_ref[hd] = m_new
            pv.append(jnp.dot(pmat.astype(BF16), vp, preferred_element_type=F32))
            alpha.append(a)
        acc_old = acc_ref[:, hp * pw:(hp + 1) * pw]
        acc_ref[:, hp * pw:(hp + 1) * pw] = (
            jnp.where(first, alpha[0], alpha[1]) * acc_old + jnp.where(first, pv[0], pv[1]))

    @pl.when(kj == qi)
    def _():
        parts = []
        for hp in range(N_HEADS // 2):
            inv = jnp.where(first, 1.0 / l_ref[2 * hp], 1.0 / l_ref[2 * hp + 1])
            parts.append(acc_ref[:, hp * pw:(hp + 1) * pw] * inv)
        o = jnp.concatenate(parts, axis=-1).astype(BF16)
        x4 = x_ref[0] + jnp.dot(o, wo_ref[...], preferred_element_type=F32)
        x4_ref[0] = x4
        h2 = _rms(x4, gffn_ref[...])
        h2_ref[0] = h2
        _route(h2, rw_ref, rb_ref, cnt_ref, idx_ref, gate_ref, rank_ref)
        cnt_out_ref[...] = cnt_ref[...]


def _attn(q, kt, v, dcc, dcr, x3, w_o, g_ffn, router_w, router_b, tq):
    b, s, d = q.shape
    h = dcc.shape[-1]
    e = router_w.shape[1]
    n = b * s
    nq = s // tq
    tk = tq
    pairs = [(i, j) for i in range(nq) for j in range(i + 1)]
    qi_arr = jnp.asarray(np.array([p[0] for p in pairs], np.int32))
    kj_arr = jnp.asarray(np.array([p[1] for p in pairs], np.int32))
    qtile = lambda bi, pr, qa, ka: (bi, qa[pr], 0)
    const2 = lambda bi, pr, qa, ka: (0, 0)
    tok = lambda bi, pr, qa, ka: (bi * nq + qa[pr], 0)
    return pl.pallas_call(
        _attn_kernel,
        grid_spec=pltpu.PrefetchScalarGridSpec(
            num_scalar_prefetch=2,
            grid=(b, len(pairs)),
            in_specs=[
                pl.BlockSpec((1, tq, d), qtile),
                pl.BlockSpec((1, d, tk), lambda bi, pr, qa, ka: (bi, 0, ka[pr])),
                pl.BlockSpec((1, tk, d), lambda bi, pr, qa, ka: (bi, ka[pr], 0)),
                pl.BlockSpec((1, tq, h), qtile),
                pl.BlockSpec((1, h, tk), lambda bi, pr, qa, ka: (bi, 0, ka[pr])),
                pl.BlockSpec((1, tq, d), qtile),
                pl.BlockSpec((d, d), const2),
                pl.BlockSpec((1, d), const2),
                pl.BlockSpec((d, e), const2),
                pl.BlockSpec((1, e), const2),
            ],
            out_specs=[
                pl.BlockSpec((1, tq, d), qtile),
                pl.BlockSpec((1, tq, d), qtile),
                pl.BlockSpec((tq, TOP_K), tok),
                pl.BlockSpec((tq, TOP_K), tok),
                pl.BlockSpec((tq, TOP_K), tok),
                pl.BlockSpec((1, e), const2),
            ],
            scratch_shapes=[
                pltpu.VMEM((N_HEADS, tq, 1), F32),
                pltpu.VMEM((N_HEADS, tq, 1), F32),
                pltpu.VMEM((tq, d), F32),
                pltpu.VMEM((1, e), F32),
            ],
        ),
        out_shape=[
            jax.ShapeDtypeStruct((b, s, d), F32),
            jax.ShapeDtypeStruct((b, s, d), F32),
            jax.ShapeDtypeStruct((n, TOP_K), jnp.int32),
            jax.ShapeDtypeStruct((n, TOP_K), F32),
            jax.ShapeDtypeStruct((n, TOP_K), jnp.int32),
            jax.ShapeDtypeStruct((1, e), F32),
        ],
        compiler_params=_cparams(2),
        name="attn",
    )(qi_arr, kj_arr, q, kt, v, dcc, dcr, x3, w_o.astype(BF16), g_ffn.reshape(1, d),
      router_w, router_b.reshape(1, e))


def _tiles(b, s):
    n = b * s
    ts = min(256, s)
    td = min(256, n)
    tm = min(512, n)
    tc = min(256, s)
    tq = min(512, s)
    return ts, td, tm, tc, tq


def _moe(h2, idx, gates, rank, counts, w_gu, b_gu, w_down, b_down, td, tm):
    n, d = h2.shape
    e = w_gu.shape[0]
    num_tiles = n * TOP_K // tm + e
    pos, tile_expert, n_used = _plan(idx, rank, counts, tm, num_tiles)
    xs = _dispatch(h2, pos, num_tiles * tm, td)
    ys = _experts(xs, tile_expert, n_used, w_gu.astype(BF16), b_gu, w_down.astype(BF16), b_down, tm)
    return pos, ys


def kernel(x, p, norm_mix, norm_ffn, norm_ple, norm_final, pool_w, pool_scale, kv_norm, w_k, w_v,
           w_fgate, b_fgate, w_q, w_o, router_w, router_b, exp_w_gu, exp_b_gu, exp_w_down,
           exp_b_down, ple_w_gate, ple_b_gate, ple_w_proj):
    b, s, d = x.shape
    n = b * s
    ts, td, tm, tc, tq = _tiles(b, s)

    x1, h2, idx, gates, rank, counts = _mix0(
        x, norm_mix[0], pool_w[0], pool_scale[0], norm_ffn[0], router_w[0], router_b[0], ts)
    pos, ys = _moe(h2.reshape(n, d), idx, gates, rank, counts,
                   exp_w_gu[0], exp_b_gu[0], exp_w_down[0], exp_b_down[0], td, tm)
    x3, q, kt, v, dcc, dcr = _combine(
        pos, ys, x1, gates, p[0], norm_ple[0], ple_w_gate[0], ple_b_gate[0], ple_w_proj[0], tc,
        kv=(kv_norm, w_k, w_v, w_fgate, b_fgate, norm_mix[1], w_q[0]))

    x4, h2, idx, gates, rank, counts = _attn(
        q, kt, v, dcc, dcr, x3, w_o[0], norm_ffn[1], router_w[1], router_b[1], tq)
    pos, ys = _moe(h2.reshape(n, d), idx, gates, rank, counts,
                   exp_w_gu[1], exp_b_gu[1], exp_w_down[1], exp_b_down[1], td, tm)
    return _combine(pos, ys, x4, gates, p[1], norm_ple[1], ple_w_gate[1], ple_b_gate[1],
                    ple_w_proj[1], tc, g_final=norm_final)
```

```python
import functools
import math

import numpy as np
import jax
import jax.numpy as jnp
from jax import lax
from jax.experimental import pallas as pl
from jax.experimental.pallas import tpu as pltpu

TOP_K = 4
POOL_WINDOWS = (2, 4, 8, 16)
POOL_HALO = 16
SWIGLU_LIMIT = 7.0
SWIGLU_ALPHA = 1.702
RMS_EPS = 1e-6
NEG_INF = -1e30
LOG2E = math.log2(math.e)
HEAD_SLOT = 128
N_DECAY_PIECES = 3
SUBLANES = 8

F32 = jnp.float32
BF16 = jnp.bfloat16
HIGHEST = lax.Precision.HIGHEST
NT_DIMS = (((1,), (1,)), ((), ()))

V7X_VMEM_BYTES = 64 * 1024 * 1024
VMEM_LIMIT = V7X_VMEM_BYTES - 8 * 1024 * 1024


def _cparams(n_axes):
    return pltpu.CompilerParams(
        dimension_semantics=("arbitrary",) * n_axes,
        vmem_limit_bytes=VMEM_LIMIT,
    )


def _rms(v, g):
    return v * lax.rsqrt(jnp.mean(v * v, axis=-1, keepdims=True) + RMS_EPS) * g


def _route(h2, rw_ref, rb_ref, gate_ref, slot_ref, cnt_ref):
    t = h2.shape[0]
    e = rw_ref.shape[1]
    logits = jnp.dot(h2, rw_ref[...], precision=HIGHEST, preferred_element_type=F32) + rb_ref[...]
    lane = lax.broadcasted_iota(jnp.int32, (t, e), 1)
    work = logits
    sels, vals = [], []
    for _ in range(TOP_K):
        m = jnp.max(work, axis=-1, keepdims=True)
        idx = jnp.min(jnp.where(work == m, lane, e), axis=-1, keepdims=True)
        sel = lane == idx
        work = jnp.where(sel, -jnp.inf, work)
        sels.append(sel)
        vals.append(m)
    exps = [jnp.exp(v - vals[0]) for v in vals]
    denom = exps[0] + exps[1] + exps[2] + exps[3]
    onehot = sels[0].astype(F32) + sels[1].astype(F32) + sels[2].astype(F32) + sels[3].astype(F32)
    row = lax.broadcasted_iota(jnp.int32, (t, t), 0)
    col = lax.broadcasted_iota(jnp.int32, (t, t), 1)
    strict_lower = (col < row).astype(BF16)
    excl = jnp.dot(strict_lower, onehot.astype(BF16), preferred_element_type=F32)
    cnt = jnp.sum(onehot, axis=0, keepdims=True)
    erow = lax.broadcasted_iota(jnp.int32, (e, e), 0)
    ecol = lax.broadcasted_iota(jnp.int32, (e, e), 1)
    run8 = jnp.floor((cnt + (SUBLANES - 1.0)) * (1.0 / SUBLANES))
    off = SUBLANES * jnp.dot(jnp.broadcast_to(run8, (SUBLANES, e)), (erow < ecol).astype(F32),
                             precision=HIGHEST, preferred_element_type=F32)[0:1, :]
    tot = excl + off
    lane_k = lax.broadcasted_iota(jnp.int32, (t, TOP_K), 1)
    gate_out = jnp.zeros((t, TOP_K), F32)
    slot_out = jnp.zeros((t, TOP_K), F32)
    for k in range(TOP_K):
        slot_k = jnp.sum(jnp.where(sels[k], tot, 0.0), axis=-1, keepdims=True)
        gate_out = jnp.where(lane_k == k, exps[k] / denom, gate_out)
        slot_out = jnp.where(lane_k == k, slot_k, slot_out)
    gate_ref[...] = gate_out
    slot_ref[...] = slot_out
    cnt_ref[...] = cnt.astype(jnp.int32)


def _mix0_kernel(x_ref, xh_ref, gmix_ref, pw_ref, ps_ref, gffn_ref, rw_ref, rb_ref,
                 x1_ref, h2_ref, gate_ref, slot_ref, cnt_ref):
    i = pl.program_id(1)
    ts = x_ref.shape[1]
    d = x_ref.shape[2]
    cg = d // len(POOL_WINDOWS)

    xt = x_ref[0]
    g = gmix_ref[...]
    h = _rms(xt, g)
    hh = jnp.where(i > 0, _rms(xh_ref[0], g), 0.0)
    hcat = jnp.concatenate([hh, h], axis=0)
    pos = i * ts + lax.broadcasted_iota(jnp.int32, (ts, 1), 0)
    outs = []
    for gi, w in enumerate(POOL_WINDOWS):
        cur = hcat[:, gi * cg:(gi + 1) * cg]
        span = 1
        while span < w:
            cur = cur + pltpu.roll(cur, span, 0)
            span *= 2
        wsum = cur[POOL_HALO:, :]
        count = jnp.minimum(pos + 1, w).astype(F32)
        diff = wsum / count - h[:, gi * cg:(gi + 1) * cg]
        outs.append(jnp.dot(diff.astype(BF16), pw_ref[gi], preferred_element_type=F32))
    mixed = jnp.concatenate(outs, axis=-1) * ps_ref[...]
    x1 = xt + mixed
    x1_ref[0] = x1
    h2 = _rms(x1, gffn_ref[...])
    h2_ref[0] = h2.astype(BF16)
    _route(h2, rw_ref, rb_ref, gate_ref, slot_ref, cnt_ref.at[0])


def _mix0(x, g_mix, pool_w, pool_scale, g_ffn, router_w, router_b, ts):
    b, s, d = x.shape
    e = router_w.shape[1]
    n = b * s
    nt = s // ts
    hb = ts // POOL_HALO
    ng = len(POOL_WINDOWS)
    cg = d // ng
    const2 = lambda bi, i: (0, 0)
    tok = lambda bi, i: (bi * nt + i, 0)
    return pl.pallas_call(
        _mix0_kernel,
        grid=(b, nt),
        in_specs=[
            pl.BlockSpec((1, ts, d), lambda bi, i: (bi, i, 0)),
            pl.BlockSpec((1, POOL_HALO, d), lambda bi, i: (bi, jnp.maximum(i * hb - 1, 0), 0)),
            pl.BlockSpec((1, d), const2),
            pl.BlockSpec((ng, cg, cg), lambda bi, i: (0, 0, 0)),
            pl.BlockSpec((1, d), const2),
            pl.BlockSpec((1, d), const2),
            pl.BlockSpec((d, e), const2),
            pl.BlockSpec((1, e), const2),
        ],
        out_specs=[
            pl.BlockSpec((1, ts, d), lambda bi, i: (bi, i, 0)),
            pl.BlockSpec((1, ts, d), lambda bi, i: (bi, i, 0)),
            pl.BlockSpec((ts, TOP_K), tok),
            pl.BlockSpec((ts, TOP_K), tok),
            pl.BlockSpec((1, 1, e), lambda bi, i: (bi * nt + i, 0, 0)),
        ],
        out_shape=[
            jax.ShapeDtypeStruct((b, s, d), F32),
            jax.ShapeDtypeStruct((b, s, d), BF16),
            jax.ShapeDtypeStruct((n, TOP_K), F32),
            jax.ShapeDtypeStruct((n, TOP_K), F32),
            jax.ShapeDtypeStruct((n // ts, 1, e), jnp.int32),
        ],
        compiler_params=_cparams(2),
        name="mix0",
    )(x, x, g_mix.reshape(1, d), pool_w.astype(BF16), pool_scale.reshape(1, d),
      g_ffn.reshape(1, d), router_w, router_b.reshape(1, e))


def _plan(cnt_tiles, tm, num_tiles):
    nt, _, e = cnt_tiles.shape
    tm8 = tm // SUBLANES
    run8 = (cnt_tiles.reshape(nt, e) + SUBLANES - 1) // SUBLANES
    total8 = jnp.sum(run8, axis=0)
    tiles_per = (total8 + tm8 - 1) // tm8
    tile_end = jnp.cumsum(tiles_per)
    tile_start = tile_end - tiles_per
    base = tile_start[None, :] * tm + SUBLANES * (jnp.cumsum(run8, axis=0) - run8)
    off = SUBLANES * (jnp.cumsum(run8, axis=1) - run8)
    pad_start = tile_start * tm + SUBLANES * total8
    pad8 = tiles_per * tm8 - total8
    n_used = tile_end[-1]
    t = jnp.arange(num_tiles, dtype=jnp.int32)
    tile_expert = jnp.sum((t[:, None] >= tile_end[None, :]).astype(jnp.int32), axis=1)
    tile_expert = jnp.minimum(tile_expert, e - 1)
    pad_total8 = jnp.sum(pad8)
    zero_full = pad_total8 // tm8 + (num_tiles - n_used)
    misc = jnp.stack([n_used, zero_full, pad_total8 % tm8])
    i32 = lambda a: a.reshape(-1).astype(jnp.int32)
    return dict(run8=i32(run8), base=i32(base), off=i32(off), tot8=i32(jnp.sum(run8, axis=1)),
                pad_start=i32(pad_start), pad8=i32(pad8), misc=i32(misc),
                tile_expert=i32(tile_expert), n_used=i32(n_used))


def _copy_run(n8, src, src_row, dst, dst_row, sem, max8):
    for bit in reversed(range(max8.bit_length())):
        size = SUBLANES << bit

        @pl.when((n8 & (1 << bit)) != 0)
        def _():
            done = ((n8 >> (bit + 1)) << (bit + 1)) * SUBLANES
            pltpu.make_async_copy(
                src.at[pl.ds(pl.multiple_of(src_row + done, SUBLANES), size), :],
                dst.at[pl.ds(pl.multiple_of(dst_row + done, SUBLANES), size), :], sem).start()


def _wait_run(n8, src, dst, sem, max8):
    for bit in reversed(range(max8.bit_length())):
        size = SUBLANES << bit

        @pl.when((n8 & (1 << bit)) != 0)
        def _():
            pltpu.make_async_copy(src.at[pl.ds(0, size), :], dst.at[pl.ds(0, size), :], sem).wait()


def _dispatch_kernel(run8_s, base_s, off_s, tot8_s, pstart_s, pad8_s, misc_s,
                     h_ref, slot_ref, eye_ref, xs_ref, xbuf, zbuf, sem, zsem):
    t = pl.program_id(0)
    nt = pl.num_programs(0)
    td = h_ref.shape[0]
    rows = xbuf.shape[1]
    tm = zbuf.shape[0]
    n_exp = pad8_s.shape[0]
    num_tiles = xs_ref.shape[0] // tm
    cur = t % 2

    def wait_step(step, s):
        _wait_run(tot8_s[step], xbuf.at[s], xs_ref, sem.at[s], rows // SUBLANES)

    @pl.when(t >= 2)
    def _():
        wait_step(t - 2, cur)

    srow = lax.dot_general(eye_ref[...], slot_ref[...], NT_DIMS, precision=HIGHEST,
                           preferred_element_type=F32)
    j = lax.broadcasted_iota(jnp.int32, (rows, td), 0).astype(F32)
    pick = j == srow[0:1, :]
    for k in range(1, TOP_K):
        pick = pick | (j == srow[k:k + 1, :])
    xbuf[cur] = jnp.dot(pick.astype(BF16), h_ref[...], preferred_element_type=F32)

    for ex in range(n_exp):
        q = t * n_exp + ex
        _copy_run(run8_s[q], xbuf.at[cur], off_s[q], xs_ref, base_s[q], sem.at[cur],
                  td // SUBLANES)

    @pl.when(t == nt - 1)
    def _():
        zbuf[...] = jnp.zeros_like(zbuf)
        tm8 = tm // SUBLANES
        for ex in range(n_exp):
            _copy_run(pad8_s[ex], zbuf, 0, xs_ref, pstart_s[ex], zsem, tm8 - 1)

        def zero_tile(tile, c):
            pltpu.make_async_copy(zbuf, xs_ref.at[pl.ds(pl.multiple_of(tile * tm, tm), tm), :],
                                  zsem).start()
            return c

        lax.fori_loop(misc_s[0], num_tiles, zero_tile, 0)

        def wait_tile(_, c):
            pltpu.make_async_copy(zbuf, xs_ref.at[pl.ds(0, tm), :], zsem).wait()
            return c

        lax.fori_loop(0, misc_s[1], wait_tile, 0)
        _wait_run(misc_s[2], zbuf, xs_ref, zsem, tm8 - 1)
        wait_step(t, cur)

        @pl.when(nt >= 2)
        def _():
            wait_step(t - 1, 1 - cur)


def _local_rows(t, n_exp):
    return t * TOP_K + SUBLANES * n_exp


def _dispatch(h2, slots, plan, rows_padded, td, tm):
    n, d = h2.shape
    n_exp = plan["pad8"].shape[0]
    return pl.pallas_call(
        _dispatch_kernel,
        grid_spec=pltpu.PrefetchScalarGridSpec(
            num_scalar_prefetch=7,
            grid=(n // td,),
            in_specs=[
                pl.BlockSpec((td, d), lambda t, *_: (t, 0)),
                pl.BlockSpec((td, TOP_K), lambda t, *_: (t, 0)),
                pl.BlockSpec((SUBLANES, TOP_K), lambda t, *_: (0, 0)),
            ],
            out_specs=pl.BlockSpec(memory_space=pl.ANY),
            scratch_shapes=[
                pltpu.VMEM((2, _local_rows(td, n_exp), d), F32),
                pltpu.VMEM((tm, d), F32),
                pltpu.SemaphoreType.DMA((2,)),
                pltpu.SemaphoreType.DMA(()),
            ],
        ),
        out_shape=jax.ShapeDtypeStruct((rows_padded, d), F32),
        compiler_params=_cparams(1),
        name="dispatch",
    )(plan["run8"], plan["base"], plan["off"], plan["tot8"], plan["pad_start"], plan["pad8"],
      plan["misc"], h2, slots, jnp.eye(SUBLANES, TOP_K, dtype=F32))


def _experts_kernel(texp_ref, nused_ref, xs_ref, wgu_ref, bgu_ref, wd_ref, bd_ref, ys_ref):
    t = pl.program_id(0)
    f = wd_ref.shape[1]

    @pl.when(t < nused_ref[0])
    def _():
        xb = xs_ref[...].astype(BF16)
        gu = jnp.dot(xb, wgu_ref[0], preferred_element_type=F32) + bgu_ref[0]
        gate = jnp.minimum(gu[:, :f], SWIGLU_LIMIT)
        up = jnp.clip(gu[:, f:], -SWIGLU_LIMIT, SWIGLU_LIMIT)
        y = (up + 1.0) * (gate * jax.nn.sigmoid(SWIGLU_ALPHA * gate))
        ys_ref[...] = jnp.dot(y.astype(BF16), wd_ref[0], preferred_element_type=F32) + bd_ref[0]

    @pl.when(t >= nused_ref[0])
    def _():
        ys_ref[...] = jnp.zeros_like(ys_ref)


def _experts(xs, plan, w_gu, b_gu, w_down, b_down, tm):
    rows, d = xs.shape
    e, _, f2 = w_gu.shape
    f = f2 // 2
    num_tiles = rows // tm
    expert = lambda t, te, nu: (te[t], 0, 0)
    return pl.pallas_call(
        _experts_kernel,
        grid_spec=pltpu.PrefetchScalarGridSpec(
            num_scalar_prefetch=2,
            grid=(num_tiles,),
            in_specs=[
                pl.BlockSpec((tm, d), lambda t, te, nu: (jnp.minimum(t, nu[0] - 1), 0)),
                pl.BlockSpec((1, d, f2), expert),
                pl.BlockSpec((1, 1, f2), expert),
                pl.BlockSpec((1, f, d), expert),
                pl.BlockSpec((1, 1, d), expert),
            ],
            out_specs=pl.BlockSpec((tm, d), lambda t, te, nu: (t, 0)),
        ),
        out_shape=jax.ShapeDtypeStruct((rows, d), F32),
        compiler_params=_cparams(1),
        name="experts",
    )(plan["tile_expert"], plan["n_used"], xs, w_gu, b_gu.reshape(e, 1, f2), w_down,
      b_down.reshape(e, 1, d))


def _combine_kernel(*refs, final, n_exp):
    if final:
        (run8_s, base_s, off_s, tot8_s, ys_hbm, x_ref, gates_ref, slot_ref, p_ref, gple_ref,
         wg_ref, bg_ref, wp_ref, gfin_ref, out_ref, ybuf, gsem) = refs
    else:
        (run8_s, base_s, off_s, tot8_s, ys_hbm, x_ref, gates_ref, slot_ref, p_ref, gple_ref,
         wg_ref, bg_ref, wp_ref,
         gkv_ref, wkp_ref, wvt_ref, wf_ref, bf_ref, gq_ref, wqt_ref,
         selk_ref, selqt_ref, onesk_ref, onesq_ref, eye_ref,
         x3_ref, qt_ref, kp_ref, vt_ref, ybuf, gsem, carry) = refs
    bi = pl.program_id(0)
    i = pl.program_id(1)
    nt = pl.num_programs(1)
    step = bi * nt + i
    nsteps = pl.num_programs(0) * nt
    tc = x_ref.shape[1]
    rows = ybuf.shape[1]
    cur = step % 2

    def fetch(s, slot):
        for ex in range(n_exp):
            q = s * n_exp + ex
            _copy_run(run8_s[q], ys_hbm, base_s[q], ybuf.at[slot], off_s[q], gsem.at[slot],
                      tc // SUBLANES)

    @pl.when(step == 0)
    def _():
        ybuf[...] = jnp.zeros_like(ybuf)
        fetch(step, cur)

    @pl.when(step + 1 < nsteps)
    def _():
        fetch(step + 1, 1 - cur)

    proj = jnp.dot(p_ref[0].astype(BF16), wp_ref[...], preferred_element_type=F32)
    x1 = x_ref[0]
    gates = gates_ref[...]
    slots = slot_ref[...]
    j = lax.broadcasted_iota(jnp.int32, (tc, rows), 1).astype(F32)
    gmat = jnp.where(j == slots[:, 0:1], gates[:, 0:1], 0.0)
    for k in range(1, TOP_K):
        gmat = gmat + jnp.where(j == slots[:, k:k + 1], gates[:, k:k + 1], 0.0)
    g_hi = gmat.astype(BF16)
    g_lo = (gmat - g_hi.astype(F32)).astype(BF16)

    _wait_run(tot8_s[step], ys_hbm, ybuf.at[cur], gsem.at[cur], rows // SUBLANES)
    yb = ybuf[cur].astype(BF16)
    moe = (jnp.dot(g_hi, yb, preferred_element_type=F32)
           + jnp.dot(g_lo, yb, preferred_element_type=F32))
    x2 = x1 + moe
    hn = _rms(x2, gple_ref[...])
    pg = jax.nn.sigmoid(jnp.dot(hn.astype(BF16), wg_ref[...], preferred_element_type=F32) + bg_ref[...])
    x3 = x2 + pg * proj

    if final:
        out_ref[0] = _rms(x3, gfin_ref[...])
        return

    x3_ref[0] = x3
    hk = _rms(x3, gkv_ref[...])
    hkb = hk.astype(BF16)
    n_heads = wf_ref.shape[1]
    vt_ref[0] = lax.dot_general(wvt_ref[...], hkb, NT_DIMS, preferred_element_type=F32).astype(BF16)

    z = jnp.dot(hk, wf_ref[...], precision=HIGHEST, preferred_element_type=F32) + bf_ref[...]
    lf = jax.nn.log_sigmoid(z) * LOG2E

    @pl.when(i == 0)
    def _():
        carry[...] = jnp.zeros_like(carry)

    row = lax.broadcasted_iota(jnp.int32, (tc, tc), 0)
    col = lax.broadcasted_iota(jnp.int32, (tc, tc), 1)
    lower_incl = (col <= row).astype(F32)
    a_c = jnp.dot(lower_incl, lf, precision=HIGHEST, preferred_element_type=F32) + carry[...]
    carry[...] = a_c[tc - 1:tc, :]
    a_r = lax.dot_general(eye_ref[...], a_c, NT_DIMS, precision=HIGHEST, preferred_element_type=F32)

    def pieces(a):
        hi = a.astype(BF16)
        r1 = a - hi.astype(F32)
        mid = r1.astype(BF16)
        lo = (r1 - mid.astype(F32)).astype(BF16)
        return hi, mid, lo

    kp = jnp.dot(hkb, wkp_ref[...], preferred_element_type=F32) + onesk_ref[...]
    for jp, pc in enumerate(pieces(-a_c)):
        kp = kp + jnp.dot(pc, selk_ref[jp], preferred_element_type=F32)
    hq = _rms(x3, gq_ref[...])
    qt = lax.dot_general(wqt_ref[...], hq.astype(BF16), NT_DIMS,
                         preferred_element_type=F32) + onesq_ref[...]
    for jp, pc in enumerate(pieces(a_r)):
        qt = qt + jnp.dot(selqt_ref[jp], pc, preferred_element_type=F32)
    for hd in range(n_heads):
        kp_ref[0, hd] = kp[:, hd * HEAD_SLOT:(hd + 1) * HEAD_SLOT].astype(BF16)
        qt_ref[0, hd] = qt[hd * HEAD_SLOT:(hd + 1) * HEAD_SLOT, :].astype(BF16)


def _combine(plan, ys, x1, gates, slots, p_i, g_ple, w_gate, b_gate, w_proj, tc, *,
             g_final=None, kv=None):
    b, s, d = x1.shape
    pd = p_i.shape[-1]
    nt = s // tc
    final = g_final is not None
    const2 = lambda bi, i, *_: (0, 0)
    tile3 = lambda bi, i, *_: (bi, i, 0)
    tok = lambda bi, i, *_: (bi * nt + i, 0)
    in_specs = [
        pl.BlockSpec(memory_space=pl.ANY),
        pl.BlockSpec((1, tc, d), tile3),
        pl.BlockSpec((tc, TOP_K), tok),
        pl.BlockSpec((tc, TOP_K), tok),
        pl.BlockSpec((1, tc, pd), tile3),
        pl.BlockSpec((1, d), const2),
        pl.BlockSpec((d, d), const2),
        pl.BlockSpec((1, d), const2),
        pl.BlockSpec((pd, d), const2),
    ]
    args = [ys, x1, gates, slots, p_i, g_ple.reshape(1, d), w_gate.astype(BF16),
            b_gate.reshape(1, d), w_proj.astype(BF16)]
    n_exp = plan["pad8"].shape[0]
    scratch = [
        pltpu.VMEM((2, _local_rows(tc, n_exp), d), F32),
        pltpu.SemaphoreType.DMA((2,)),
    ]
    if final:
        in_specs.append(pl.BlockSpec((1, d), const2))
        args.append(g_final.reshape(1, d))
        out_specs = pl.BlockSpec((1, tc, d), tile3)
        out_shape = jax.ShapeDtypeStruct((b, s, d), F32)
    else:
        g_kv, w_k, w_v, w_f, b_f, g_q, w_q = kv
        h = w_f.shape[1]
        dh = d // h
        hs = h * HEAD_SLOT
        npc = N_DECAY_PIECES

        def slotted(w):
            w3 = jnp.pad(w.reshape(d, h, dh), ((0, 0), (0, 0), (0, HEAD_SLOT - dh)))
            return w3.reshape(d, hs)

        selk = np.zeros((npc, h, hs), np.float32)
        selqt = np.zeros((npc, hs, h), np.float32)
        onesk = np.zeros((1, hs), np.float32)
        onesq = np.zeros((hs, 1), np.float32)
        for hd in range(h):
            for jp in range(npc):
                selqt[jp, hd * HEAD_SLOT + dh + jp, hd] = 1.0
                onesk[0, hd * HEAD_SLOT + dh + jp] = 1.0
                selk[jp, hd, hd * HEAD_SLOT + dh + npc + jp] = 1.0
                onesq[hd * HEAD_SLOT + dh + npc + jp, 0] = 1.0
        const3 = lambda bi, i, *_: (0, 0, 0)
        in_specs += [
            pl.BlockSpec((1, d), const2),
            pl.BlockSpec((d, hs), const2),
            pl.BlockSpec((d, d), const2),
            pl.BlockSpec((d, h), const2),
            pl.BlockSpec((1, h), const2),
            pl.BlockSpec((1, d), const2),
            pl.BlockSpec((hs, d), const2),
            pl.BlockSpec((npc, h, hs), const3),
            pl.BlockSpec((npc, hs, h), const3),
            pl.BlockSpec((1, hs), const2),
            pl.BlockSpec((hs, 1), const2),
            pl.BlockSpec((h, h), const2),
        ]
        qscale = LOG2E / math.sqrt(dh)
        args += [g_kv.reshape(1, d), slotted(w_k).astype(BF16), w_v.T.astype(BF16), w_f,
                 b_f.reshape(1, h), g_q.reshape(1, d), slotted(w_q * qscale).T.astype(BF16),
                 jnp.asarray(selk, BF16), jnp.asarray(selqt, BF16), jnp.asarray(onesk),
                 jnp.asarray(onesq), jnp.eye(h, dtype=F32)]
        out_specs = [
            pl.BlockSpec((1, tc, d), tile3),
            pl.BlockSpec((1, h, HEAD_SLOT, tc), lambda bi, i, *_: (bi, 0, 0, i)),
            pl.BlockSpec((1, h, tc, HEAD_SLOT), lambda bi, i, *_: (bi, 0, i, 0)),
            pl.BlockSpec((1, d, tc), lambda bi, i, *_: (bi, 0, i)),
        ]
        out_shape = [
            jax.ShapeDtypeStruct((b, s, d), F32),
            jax.ShapeDtypeStruct((b, h, HEAD_SLOT, s), BF16),
            jax.ShapeDtypeStruct((b, h, s, HEAD_SLOT), BF16),
            jax.ShapeDtypeStruct((b, d, s), BF16),
        ]
        scratch += [pltpu.VMEM((1, h), F32)]
    return pl.pallas_call(
        functools.partial(_combine_kernel, final=final, n_exp=n_exp),
        grid_spec=pltpu.PrefetchScalarGridSpec(
            num_scalar_prefetch=4,
            grid=(b, nt),
            in_specs=in_specs,
            out_specs=out_specs,
            scratch_shapes=scratch,
        ),
        out_shape=out_shape,
        compiler_params=_cparams(2),
        name="combine_final" if final else "combine_kv",
    )(plan["run8"], plan["base"], plan["off"], plan["tot8"], *args)


def _attn_kernel(qi_ref, kj_ref, qt_ref, kp_ref, vt_ref, x_ref, wo_ref,
                 gffn_ref, rw_ref, rb_ref,
                 x4_ref, h2_ref, gate_ref, slot_ref, cnt_ref,
                 m_ref, l_ref, acc_ref, st_ref):
    pr = pl.program_id(1)
    qi = qi_ref[pr]
    kj = kj_ref[pr]
    n_heads = qt_ref.shape[1]
    tq = qt_ref.shape[3]
    tk = kp_ref.shape[2]
    dh = vt_ref.shape[1] // n_heads
    n_sub = cnt_ref.shape[0]
    tr = tq // n_sub

    @pl.when(kj == 0)
    def _():
        m_ref[...] = jnp.full_like(m_ref, NEG_INF)
        l_ref[...] = jnp.zeros_like(l_ref)
        acc_ref[...] = jnp.zeros_like(acc_ref)

    def logits(hd):
        st_ref[hd % 2] = jnp.dot(kp_ref[0, hd], qt_ref[0, hd], preferred_element_type=F32)

    def head_step(hd, masked):
        st = st_ref[hd % 2]
        if masked:
            krow = lax.broadcasted_iota(jnp.int32, (tk, tq), 0)
            qcol = lax.broadcasted_iota(jnp.int32, (tk, tq), 1)
            st = jnp.where(krow <= qcol, st, NEG_INF)
        m_old = m_ref[hd]
        m_new = jnp.maximum(m_old, jnp.max(st, axis=0, keepdims=True))
        a = jnp.exp2(m_old - m_new)
        pt = jnp.exp2(st - m_new)
        l_ref[hd] = a * l_ref[hd] + jnp.sum(pt, axis=0, keepdims=True)
        m_ref[hd] = m_new
        pv = jnp.dot(vt_ref[0, hd * dh:(hd + 1) * dh, :], pt.astype(BF16),
                     preferred_element_type=F32)
        acc_ref[hd * dh:(hd + 1) * dh, :] = a * acc_ref[hd * dh:(hd + 1) * dh, :] + pv

    def all_heads(masked):
        logits(0)
        for hd in range(n_heads):
            if hd + 1 < n_heads:
                logits(hd + 1)
            head_step(hd, masked)

    @pl.when(kj < qi)
    def _():
        all_heads(False)

    @pl.when(kj == qi)
    def _():
        all_heads(True)
        parts = [acc_ref[hd * dh:(hd + 1) * dh, :] * (1.0 / l_ref[hd]) for hd in range(n_heads)]
        o = jnp.concatenate(parts, axis=0).T.astype(BF16)
        x4 = x_ref[0] + jnp.dot(o, wo_ref[...], preferred_element_type=F32)
        x4_ref[0] = x4
        h2 = _rms(x4, gffn_ref[...])
        h2_ref[0] = h2.astype(BF16)
        for sb in range(n_sub):
            rs = slice(sb * tr, (sb + 1) * tr)
            _route(h2[rs], rw_ref, rb_ref, gate_ref.at[rs], slot_ref.at[rs], cnt_ref.at[sb])


def _attn(qt, kp, vt, x3, w_o, g_ffn, router_w, router_b, tq, tr):
    b, s, d = x3.shape
    h = qt.shape[1]
    e = router_w.shape[1]
    n = b * s
    nq = s // tq
    tk = tq
    n_sub = tq // tr
    pairs = [(i, j) for i in range(nq) for j in range(i + 1)]
    qi_arr = jnp.asarray(np.array([p[0] for p in pairs], np.int32))
    kj_arr = jnp.asarray(np.array([p[1] for p in pairs], np.int32))
    qtile = lambda bi, pr, qa, ka: (bi, qa[pr], 0)
    const2 = lambda bi, pr, qa, ka: (0, 0)
    tok = lambda bi, pr, qa, ka: (bi * nq + qa[pr], 0)
    return pl.pallas_call(
        _attn_kernel,
        grid_spec=pltpu.PrefetchScalarGridSpec(
            num_scalar_prefetch=2,
            grid=(b, len(pairs)),
            in_specs=[
                pl.BlockSpec((1, h, HEAD_SLOT, tq), lambda bi, pr, qa, ka: (bi, 0, 0, qa[pr])),
                pl.BlockSpec((1, h, tk, HEAD_SLOT), lambda bi, pr, qa, ka: (bi, 0, ka[pr], 0)),
                pl.BlockSpec((1, d, tk), lambda bi, pr, qa, ka: (bi, 0, ka[pr])),
                pl.BlockSpec((1, tq, d), qtile),
                pl.BlockSpec((d, d), const2),
                pl.BlockSpec((1, d), const2),
                pl.BlockSpec((d, e), const2),
                pl.BlockSpec((1, e), const2),
            ],
            out_specs=[
                pl.BlockSpec((1, tq, d), qtile),
                pl.BlockSpec((1, tq, d), qtile),
                pl.BlockSpec((tq, TOP_K), tok),
                pl.BlockSpec((tq, TOP_K), tok),
                pl.BlockSpec((n_sub, 1, e), lambda bi, pr, qa, ka: (bi * nq + qa[pr], 0, 0)),
            ],
            scratch_shapes=[
                pltpu.VMEM((h, 1, tq), F32),
                pltpu.VMEM((h, 1, tq), F32),
                pltpu.VMEM((d, tq), F32),
                pltpu.VMEM((2, tk, tq), F32),
            ],
        ),
        out_shape=[
            jax.ShapeDtypeStruct((b, s, d), F32),
            jax.ShapeDtypeStruct((b, s, d), BF16),
            jax.ShapeDtypeStruct((n, TOP_K), F32),
            jax.ShapeDtypeStruct((n, TOP_K), F32),
            jax.ShapeDtypeStruct((n // tr, 1, e), jnp.int32),
        ],
        compiler_params=_cparams(2),
        name="attn",
    )(qi_arr, kj_arr, qt, kp, vt, x3, w_o.astype(BF16), g_ffn.reshape(1, d),
      router_w, router_b.reshape(1, e))


def _tiles(b, s):
    n = b * s
    tr = min(256, s)
    tm = min(512, n)
    tq = min(512, s)
    return tr, tm, tq


def _moe(h2, slots, cnt_tiles, w_gu, b_gu, w_down, b_down, tr, tm):
    n, d = h2.shape
    e = w_gu.shape[0]
    num_tiles = _local_rows(tr, e) * (n // tr) // tm + e
    plan = _plan(cnt_tiles, tm, num_tiles)
    xs = _dispatch(h2, slots, plan, num_tiles * tm, tr, tm)
    ys = _experts(xs, plan, w_gu.astype(BF16), b_gu, w_down.astype(BF16), b_down, tm)
    return plan, ys


def kernel(x, p, norm_mix, norm_ffn, norm_ple, norm_final, pool_w, pool_scale, kv_norm, w_k, w_v,
           w_fgate, b_fgate, w_q, w_o, router_w, router_b, exp_w_gu, exp_b_gu, exp_w_down,
           exp_b_down, ple_w_gate, ple_b_gate, ple_w_proj):
    b, s, d = x.shape
    n = b * s
    tr, tm, tq = _tiles(b, s)

    x1, h2, gates, slots, cnt_tiles = _mix0(
        x, norm_mix[0], pool_w[0], pool_scale[0], norm_ffn[0], router_w[0], router_b[0], tr)
    plan, ys = _moe(h2.reshape(n, d), slots, cnt_tiles,
                    exp_w_gu[0], exp_b_gu[0], exp_w_down[0], exp_b_down[0], tr, tm)
    x3, qt, kp, vt = _combine(
        plan, ys, x1, gates, slots, p[0], norm_ple[0], ple_w_gate[0], ple_b_gate[0],
        ple_w_proj[0], tr, kv=(kv_norm, w_k, w_v, w_fgate, b_fgate, norm_mix[1], w_q[0]))

    x4, h2, gates, slots, cnt_tiles = _attn(
        qt, kp, vt, x3, w_o[0], norm_ffn[1], router_w[1], router_b[1], tq, tr)
    plan, ys = _moe(h2.reshape(n, d), slots, cnt_tiles,
                    exp_w_gu[1], exp_b_gu[1], exp_w_down[1], exp_b_down[1], tr, tm)
    return _combine(plan, ys, x4, gates, slots, p[1], norm_ple[1], ple_w_gate[1], ple_b_gate[1],
                    ple_w_proj[1], tr, g_final=norm_final)
```

```python
import functools
import math

import numpy as np
import jax
import jax.numpy as jnp
from jax import lax
from jax.experimental import pallas as pl
from jax.experimental.pallas import tpu as pltpu

TOP_K = 4
POOL_WINDOWS = (2, 4, 8, 16)
POOL_HALO = 16
SWIGLU_LIMIT = 7.0
SWIGLU_ALPHA = 1.702
RMS_EPS = 1e-6
NEG_INF = -1e30
LOG2E = math.log2(math.e)
HEAD_SLOT = 128
N_DECAY_PIECES = 3
SUBLANES = 8
LONG_RUN_BIT = 3

F32 = jnp.float32
BF16 = jnp.bfloat16
NT_DIMS = (((1,), (1,)), ((), ()))

V7X_VMEM_BYTES = 64 * 1024 * 1024
VMEM_LIMIT = V7X_VMEM_BYTES - 8 * 1024 * 1024


def _cparams(n_axes):
    return pltpu.CompilerParams(
        dimension_semantics=("arbitrary",) * n_axes,
        vmem_limit_bytes=VMEM_LIMIT,
    )


def _rms(v, g):
    return v * lax.rsqrt(jnp.mean(v * v, axis=-1, keepdims=True) + RMS_EPS) * g


def _split3(a):
    hi = a.astype(BF16)
    r1 = a - hi.astype(F32)
    mid = r1.astype(BF16)
    lo = (r1 - mid.astype(F32)).astype(BF16)
    return hi, mid, lo


def _mm(a, b, dims):
    if dims is None:
        return jnp.dot(a, b, preferred_element_type=F32)
    return lax.dot_general(a, b, dims, preferred_element_type=F32)


def _dot_x3(a, b, dims=None):
    a_hi = a.astype(BF16)
    a_lo = (a - a_hi.astype(F32)).astype(BF16)
    b_hi = b.astype(BF16)
    b_lo = (b - b_hi.astype(F32)).astype(BF16)
    return _mm(a_hi, b_hi, dims) + _mm(a_hi, b_lo, dims) + _mm(a_lo, b_hi, dims)


def _dot_sel(sel, x, dims=None, sel_first=True):
    sel = sel.astype(BF16)
    out = None
    for piece in _split3(x):
        term = _mm(sel, piece, dims) if sel_first else _mm(piece, sel, dims)
        out = term if out is None else out + term
    return out


def _lower_tri(t, strict):
    r = np.arange(t)
    keep = r[None, :] < r[:, None] if strict else r[None, :] <= r[:, None]
    return jnp.asarray(keep, BF16)


def _route(h2, rw_ref, rb_ref, tri_ref, gate_ref, slot_ref, cnt_ref):
    t = h2.shape[0]
    e = rw_ref.shape[1]
    logits = _dot_x3(h2, rw_ref[...]) + rb_ref[...]
    lane = lax.broadcasted_iota(jnp.int32, (t, e), 1).astype(F32)
    work = logits
    sels, vals = [], []
    for _ in range(TOP_K):
        m = jnp.max(work, axis=-1, keepdims=True)
        idx = jnp.min(jnp.where(work == m, lane, float(e)), axis=-1, keepdims=True)
        sel = lane == idx
        work = jnp.where(sel, -jnp.inf, work)
        sels.append(sel)
        vals.append(m)
    exps = [jnp.exp(v - vals[0]) for v in vals]
    denom = exps[0] + exps[1] + exps[2] + exps[3]
    onehot = sels[0].astype(F32) + sels[1].astype(F32) + sels[2].astype(F32) + sels[3].astype(F32)
    excl = jnp.dot(tri_ref[...], onehot.astype(BF16), preferred_element_type=F32)
    cnt = jnp.sum(onehot, axis=0, keepdims=True)
    erow = lax.broadcasted_iota(jnp.int32, (e, e), 0)
    ecol = lax.broadcasted_iota(jnp.int32, (e, e), 1)
    run8 = jnp.floor((cnt + (SUBLANES - 1.0)) * (1.0 / SUBLANES))
    off = SUBLANES * jnp.dot(jnp.broadcast_to(run8, (SUBLANES, e)).astype(BF16),
                             (erow < ecol).astype(BF16), preferred_element_type=F32)[0:1, :]
    tot = excl + off
    lane_k = lax.broadcasted_iota(jnp.int32, (t, TOP_K), 1)
    gate_out = jnp.zeros((t, TOP_K), F32)
    slot_out = jnp.zeros((t, TOP_K), F32)
    for k in range(TOP_K):
        slot_k = jnp.sum(jnp.where(sels[k], tot, 0.0), axis=-1, keepdims=True)
        gate_out = jnp.where(lane_k == k, exps[k] / denom, gate_out)
        slot_out = jnp.where(lane_k == k, slot_k, slot_out)
    gate_ref[...] = gate_out
    slot_ref[...] = slot_out
    cnt_ref[...] = cnt.astype(jnp.int32)


def _mix0_kernel(x_ref, xh_ref, gmix_ref, pw_ref, ps_ref, gffn_ref, rw_ref, rb_ref, tri_ref,
                 x1_ref, h2_ref, gate_ref, slot_ref, cnt_ref):
    i = pl.program_id(1)
    ts = x_ref.shape[1]
    d = x_ref.shape[2]
    cg = d // len(POOL_WINDOWS)

    xt = x_ref[0]
    g = gmix_ref[...]
    h = _rms(xt, g)
    hh = jnp.where(i > 0, _rms(xh_ref[0], g), 0.0)
    hcat = jnp.concatenate([hh, h], axis=0)
    pos = i * ts + lax.broadcasted_iota(jnp.int32, (ts, 1), 0)
    outs = []
    for gi, w in enumerate(POOL_WINDOWS):
        cur = hcat[:, gi * cg:(gi + 1) * cg]
        span = 1
        while span < w:
            cur = cur + pltpu.roll(cur, span, 0)
            span *= 2
        wsum = cur[POOL_HALO:, :]
        count = jnp.minimum(pos + 1, w).astype(F32)
        diff = wsum / count - h[:, gi * cg:(gi + 1) * cg]
        outs.append(jnp.dot(diff.astype(BF16), pw_ref[gi], preferred_element_type=F32))
    mixed = jnp.concatenate(outs, axis=-1) * ps_ref[...]
    x1 = xt + mixed
    x1_ref[0] = x1
    h2 = _rms(x1, gffn_ref[...])
    h2_ref[0] = h2.astype(BF16)
    _route(h2, rw_ref, rb_ref, tri_ref, gate_ref, slot_ref, cnt_ref.at[0])


def _mix0(x, g_mix, pool_w, pool_scale, g_ffn, router_w, router_b, ts):
    b, s, d = x.shape
    e = router_w.shape[1]
    n = b * s
    nt = s // ts
    hb = ts // POOL_HALO
    ng = len(POOL_WINDOWS)
    cg = d // ng
    const2 = lambda bi, i: (0, 0)
    tok = lambda bi, i: (bi * nt + i, 0)
    return pl.pallas_call(
        _mix0_kernel,
        grid=(b, nt),
        in_specs=[
            pl.BlockSpec((1, ts, d), lambda bi, i: (bi, i, 0)),
            pl.BlockSpec((1, POOL_HALO, d), lambda bi, i: (bi, jnp.maximum(i * hb - 1, 0), 0)),
            pl.BlockSpec((1, d), const2),
            pl.BlockSpec((ng, cg, cg), lambda bi, i: (0, 0, 0)),
            pl.BlockSpec((1, d), const2),
            pl.BlockSpec((1, d), const2),
            pl.BlockSpec((d, e), const2),
            pl.BlockSpec((1, e), const2),
            pl.BlockSpec((ts, ts), const2),
        ],
        out_specs=[
            pl.BlockSpec((1, ts, d), lambda bi, i: (bi, i, 0)),
            pl.BlockSpec((1, ts, d), lambda bi, i: (bi, i, 0)),
            pl.BlockSpec((ts, TOP_K), tok),
            pl.BlockSpec((ts, TOP_K), tok),
            pl.BlockSpec((1, 1, e), lambda bi, i: (bi * nt + i, 0, 0)),
        ],
        out_shape=[
            jax.ShapeDtypeStruct((b, s, d), F32),
            jax.ShapeDtypeStruct((b, s, d), BF16),
            jax.ShapeDtypeStruct((n, TOP_K), F32),
            jax.ShapeDtypeStruct((n, TOP_K), F32),
            jax.ShapeDtypeStruct((n // ts, 1, e), jnp.int32),
        ],
        compiler_params=_cparams(2),
        name="mix0",
    )(x, x, g_mix.reshape(1, d), pool_w.astype(BF16), pool_scale.reshape(1, d),
      g_ffn.reshape(1, d), router_w, router_b.reshape(1, e), _lower_tri(ts, strict=True))


def _plan(cnt_tiles, tm, num_tiles):
    nt, _, e = cnt_tiles.shape
    tm8 = tm // SUBLANES
    run8 = (cnt_tiles.reshape(nt, e) + SUBLANES - 1) // SUBLANES
    total8 = jnp.sum(run8, axis=0)
    tiles_per = (total8 + tm8 - 1) // tm8
    tile_end = jnp.cumsum(tiles_per)
    tile_start = tile_end - tiles_per
    base = tile_start[None, :] * tm + SUBLANES * (jnp.cumsum(run8, axis=0) - run8)
    off = SUBLANES * (jnp.cumsum(run8, axis=1) - run8)
    pad_start = tile_start * tm + SUBLANES * total8
    pad8 = tiles_per * tm8 - total8
    n_used = tile_end[-1]
    t = jnp.arange(num_tiles, dtype=jnp.int32)
    tile_expert = jnp.sum((t[:, None] >= tile_end[None, :]).astype(jnp.int32), axis=1)
    tile_expert = jnp.minimum(tile_expert, e - 1)
    pad_total8 = jnp.sum(pad8)
    zero_full = pad_total8 // tm8 + (num_tiles - n_used)
    misc = jnp.stack([n_used, zero_full, pad_total8 % tm8])
    i32 = lambda a: a.reshape(-1).astype(jnp.int32)
    return dict(run8=i32(run8), base=i32(base), off=i32(off), tot8=i32(jnp.sum(run8, axis=1)),
                pad_start=i32(pad_start), pad8=i32(pad8), misc=i32(misc),
                tile_expert=i32(tile_expert), n_used=i32(n_used))


def _copy_run(n8, src, src_row, dst, dst_row, sem, max8):
    def copy_bit(bit):
        size = SUBLANES << bit

        @pl.when((n8 & (1 << bit)) != 0)
        def _():
            done = ((n8 >> (bit + 1)) << (bit + 1)) * SUBLANES
            pltpu.make_async_copy(
                src.at[pl.ds(pl.multiple_of(src_row + done, SUBLANES), size), :],
                dst.at[pl.ds(pl.multiple_of(dst_row + done, SUBLANES), size), :], sem).start()

    bits = list(reversed(range(max8.bit_length())))
    long_bits = [bit for bit in bits if bit >= LONG_RUN_BIT]
    if long_bits:
        @pl.when(n8 >= (1 << LONG_RUN_BIT))
        def _():
            for bit in long_bits:
                copy_bit(bit)
    for bit in bits:
        if bit < LONG_RUN_BIT:
            copy_bit(bit)


def _wait_run(n8, src, dst, sem, max8):
    for bit in reversed(range(max8.bit_length())):
        size = SUBLANES << bit

        @pl.when((n8 & (1 << bit)) != 0)
        def _():
            pltpu.make_async_copy(src.at[pl.ds(0, size), :], dst.at[pl.ds(0, size), :], sem).wait()


def _dispatch_kernel(run8_s, base_s, off_s, tot8_s, pstart_s, pad8_s, misc_s,
                     h_ref, slot_ref, eye_ref, xs_ref, xbuf, zbuf, sem, zsem):
    t = pl.program_id(0)
    nt = pl.num_programs(0)
    td = h_ref.shape[0]
    rows = xbuf.shape[1]
    tm = zbuf.shape[0]
    n_exp = pad8_s.shape[0]
    num_tiles = xs_ref.shape[0] // tm
    cur = t % 2

    def wait_step(step, s):
        _wait_run(tot8_s[step], xbuf.at[s], xs_ref, sem.at[s], rows // SUBLANES)

    @pl.when(t >= 2)
    def _():
        wait_step(t - 2, cur)

    srow = _dot_sel(eye_ref[...], slot_ref[...], NT_DIMS)
    j = lax.broadcasted_iota(jnp.int32, (rows, td), 0).astype(F32)
    pick = j == srow[0:1, :]
    for k in range(1, TOP_K):
        pick = pick | (j == srow[k:k + 1, :])
    xbuf[cur] = jnp.dot(pick.astype(BF16), h_ref[...], preferred_element_type=F32)

    for ex in range(n_exp):
        q = t * n_exp + ex
        _copy_run(run8_s[q], xbuf.at[cur], off_s[q], xs_ref, base_s[q], sem.at[cur],
                  td // SUBLANES)

    @pl.when(t == nt - 1)
    def _():
        zbuf[...] = jnp.zeros_like(zbuf)
        tm8 = tm // SUBLANES
        for ex in range(n_exp):
            _copy_run(pad8_s[ex], zbuf, 0, xs_ref, pstart_s[ex], zsem, tm8 - 1)

        def zero_tile(tile, c):
            pltpu.make_async_copy(zbuf, xs_ref.at[pl.ds(pl.multiple_of(tile * tm, tm), tm), :],
                                  zsem).start()
            return c

        lax.fori_loop(misc_s[0], num_tiles, zero_tile, 0)

        def wait_tile(_, c):
            pltpu.make_async_copy(zbuf, xs_ref.at[pl.ds(0, tm), :], zsem).wait()
            return c

        lax.fori_loop(0, misc_s[1], wait_tile, 0)
        _wait_run(misc_s[2], zbuf, xs_ref, zsem, tm8 - 1)
        wait_step(t, cur)

        @pl.when(nt >= 2)
        def _():
            wait_step(t - 1, 1 - cur)


def _local_rows(t, n_exp):
    return t * TOP_K + SUBLANES * n_exp


def _dispatch(h2, slots, plan, rows_padded, td, tm):
    n, d = h2.shape
    n_exp = plan["pad8"].shape[0]
    return pl.pallas_call(
        _dispatch_kernel,
        grid_spec=pltpu.PrefetchScalarGridSpec(
            num_scalar_prefetch=7,
            grid=(n // td,),
            in_specs=[
                pl.BlockSpec((td, d), lambda t, *_: (t, 0)),
                pl.BlockSpec((td, TOP_K), lambda t, *_: (t, 0)),
                pl.BlockSpec((SUBLANES, TOP_K), lambda t, *_: (0, 0)),
            ],
            out_specs=pl.BlockSpec(memory_space=pl.ANY),
            scratch_shapes=[
                pltpu.VMEM((2, _local_rows(td, n_exp), d), F32),
                pltpu.VMEM((tm, d), F32),
                pltpu.SemaphoreType.DMA((2,)),
                pltpu.SemaphoreType.DMA(()),
            ],
        ),
        out_shape=jax.ShapeDtypeStruct((rows_padded, d), F32),
        compiler_params=_cparams(1),
        name="dispatch",
    )(plan["run8"], plan["base"], plan["off"], plan["tot8"], plan["pad_start"], plan["pad8"],
      plan["misc"], h2, slots, jnp.eye(SUBLANES, TOP_K, dtype=F32))


def _experts_kernel(texp_ref, nused_ref, xs_ref, wgu_ref, bgu_ref, wd_ref, bd_ref, ys_ref,
                    wgu_bf, wd_bf):
    t = pl.program_id(0)
    f = wd_ref.shape[2]

    @pl.when((t == 0) | (texp_ref[t] != texp_ref[jnp.maximum(t - 1, 0)]))
    def _():
        wgu_bf[...] = wgu_ref[0, 0].astype(BF16)
        wd_bf[...] = wd_ref[0, 0].astype(BF16)

    @pl.when(t < nused_ref[0])
    def _():
        xb = xs_ref[...].astype(BF16)
        gu = jnp.dot(xb, wgu_bf[...], preferred_element_type=F32) + bgu_ref[0]
        gate = jnp.minimum(gu[:, :f], SWIGLU_LIMIT)
        up = jnp.clip(gu[:, f:], -SWIGLU_LIMIT, SWIGLU_LIMIT)
        y = (up + 1.0) * (gate * jax.nn.sigmoid(SWIGLU_ALPHA * gate))
        ys_ref[...] = jnp.dot(y.astype(BF16), wd_bf[...], preferred_element_type=F32) + bd_ref[0]

    @pl.when(t >= nused_ref[0])
    def _():
        ys_ref[...] = jnp.zeros_like(ys_ref)


def _experts(xs, plan, w_gu, b_gu, w_down, b_down, layer, tm):
    rows, d = xs.shape
    _, e, _, f2 = w_gu.shape
    f = f2 // 2
    num_tiles = rows // tm
    expert = lambda t, te, nu: (te[t], 0, 0)
    layer_expert = lambda t, te, nu: (layer, te[t], 0, 0)
    return pl.pallas_call(
        _experts_kernel,
        grid_spec=pltpu.PrefetchScalarGridSpec(
            num_scalar_prefetch=2,
            grid=(num_tiles,),
            in_specs=[
                pl.BlockSpec((tm, d), lambda t, te, nu: (jnp.minimum(t, nu[0] - 1), 0)),
                pl.BlockSpec((1, 1, d, f2), layer_expert),
                pl.BlockSpec((1, 1, f2), expert),
                pl.BlockSpec((1, 1, f, d), layer_expert),
                pl.BlockSpec((1, 1, d), expert),
            ],
            out_specs=pl.BlockSpec((tm, d), lambda t, te, nu: (t, 0)),
            scratch_shapes=[pltpu.VMEM((d, f2), BF16), pltpu.VMEM((f, d), BF16)],
        ),
        out_shape=jax.ShapeDtypeStruct((rows, d), F32),
        compiler_params=_cparams(1),
        name="experts",
    )(plan["tile_expert"], plan["n_used"], xs, w_gu, b_gu.reshape(e, 1, f2), w_down,
      b_down.reshape(e, 1, d))


def _combine_kernel(*refs, final, n_exp):
    if final:
        (run8_s, base_s, off_s, tot8_s, ys_hbm, x_ref, gates_ref, slot_ref, p_ref, gple_ref,
         wg_ref, bg_ref, wp_ref, gfin_ref, out_ref, ybuf, gsem) = refs
    else:
        (run8_s, base_s, off_s, tot8_s, ys_hbm, x_ref, gates_ref, slot_ref, p_ref, gple_ref,
         wg_ref, bg_ref, wp_ref,
         gkv_ref, wkp_ref, wvt_ref, wf_ref, bf_ref, gq_ref, wqt_ref,
         selk_ref, selqt_ref, onesk_ref, onesq_ref, eye_ref, tri_ref,
         x3_ref, qt_ref, kp_ref, vt_ref, ybuf, gsem, carry) = refs
    bi = pl.program_id(0)
    i = pl.program_id(1)
    nt = pl.num_programs(1)
    step = bi * nt + i
    nsteps = pl.num_programs(0) * nt
    tc = x_ref.shape[1]
    rows = ybuf.shape[1]
    cur = step % 2

    def fetch(s, slot):
        for ex in range(n_exp):
            q = s * n_exp + ex
            _copy_run(run8_s[q], ys_hbm, base_s[q], ybuf.at[slot], off_s[q], gsem.at[slot],
                      tc // SUBLANES)

    @pl.when(step == 0)
    def _():
        ybuf[...] = jnp.zeros_like(ybuf)
        fetch(step, cur)

    @pl.when(step + 1 < nsteps)
    def _():
        fetch(step + 1, 1 - cur)

    proj = jnp.dot(p_ref[0].astype(BF16), wp_ref[...], preferred_element_type=F32)
    x1 = x_ref[0]
    gates = gates_ref[...]
    slots = slot_ref[...]
    j = lax.broadcasted_iota(jnp.int32, (tc, rows), 1).astype(F32)
    gmat = jnp.where(j == slots[:, 0:1], gates[:, 0:1], 0.0)
    for k in range(1, TOP_K):
        gmat = gmat + jnp.where(j == slots[:, k:k + 1], gates[:, k:k + 1], 0.0)
    g_hi = gmat.astype(BF16)
    g_lo = (gmat - g_hi.astype(F32)).astype(BF16)

    _wait_run(tot8_s[step], ys_hbm, ybuf.at[cur], gsem.at[cur], rows // SUBLANES)
    yb = ybuf[cur].astype(BF16)
    moe = (jnp.dot(g_hi, yb, preferred_element_type=F32)
           + jnp.dot(g_lo, yb, preferred_element_type=F32))
    x2 = x1 + moe
    hn = _rms(x2, gple_ref[...])
    pg = jax.nn.sigmoid(jnp.dot(hn.astype(BF16), wg_ref[...], preferred_element_type=F32) + bg_ref[...])
    x3 = x2 + pg * proj

    if final:
        out_ref[0] = _rms(x3, gfin_ref[...])
        return

    x3_ref[0] = x3
    hk = _rms(x3, gkv_ref[...])
    hkb = hk.astype(BF16)
    n_heads = kp_ref.shape[1]
    vt_ref[0] = lax.dot_general(wvt_ref[...], hkb, NT_DIMS, preferred_element_type=F32).astype(BF16)

    z = _dot_x3(hk, wf_ref[...]) + bf_ref[...]
    lf = jax.nn.log_sigmoid(z) * LOG2E

    @pl.when(i == 0)
    def _():
        carry[...] = jnp.zeros_like(carry)

    a_c = _dot_sel(tri_ref[...], lf) + carry[...]
    carry[...] = a_c[tc - 1:tc, :]
    a_r = _dot_sel(eye_ref[...], a_c, NT_DIMS)

    def piece_of_copy(a, copy_index):
        out = None
        for jp, pc in enumerate(_split3(a)):
            out = pc if out is None else jnp.where(copy_index == jp, pc, out)
        return out

    pk = piece_of_copy(-a_c, lax.broadcasted_iota(jnp.int32, a_c.shape, 1) // n_heads)
    kp = (jnp.dot(hkb, wkp_ref[...], preferred_element_type=F32) + onesk_ref[...]
          + jnp.dot(pk, selk_ref[...], preferred_element_type=F32))
    hq = _rms(x3, gq_ref[...])
    pq = piece_of_copy(a_r, lax.broadcasted_iota(jnp.int32, a_r.shape, 0) // n_heads)
    qt = (lax.dot_general(wqt_ref[...], hq.astype(BF16), NT_DIMS, preferred_element_type=F32)
          + onesq_ref[...] + jnp.dot(selqt_ref[...], pq, preferred_element_type=F32))
    for hd in range(n_heads):
        kp_ref[0, hd] = kp[:, hd * HEAD_SLOT:(hd + 1) * HEAD_SLOT].astype(BF16)
        qt_ref[0, hd] = qt[hd * HEAD_SLOT:(hd + 1) * HEAD_SLOT, :].astype(BF16)


def _combine(plan, ys, x1, gates, slots, p_all, layer, g_ple, w_gate, b_gate, w_proj, tc, *,
             g_final=None, kv=None):
    b, s, d = x1.shape
    pd = p_all.shape[-1]
    p_rows = p_all.reshape(-1, s, pd)
    nt = s // tc
    final = g_final is not None
    const2 = lambda bi, i, *_: (0, 0)
    tile3 = lambda bi, i, *_: (bi, i, 0)
    tok = lambda bi, i, *_: (bi * nt + i, 0)
    in_specs = [
        pl.BlockSpec(memory_space=pl.ANY),
        pl.BlockSpec((1, tc, d), tile3),
        pl.BlockSpec((tc, TOP_K), tok),
        pl.BlockSpec((tc, TOP_K), tok),
        pl.BlockSpec((1, tc, pd), lambda bi, i, *_: (layer * b + bi, i, 0)),
        pl.BlockSpec((1, d), const2),
        pl.BlockSpec((d, d), const2),
        pl.BlockSpec((1, d), const2),
        pl.BlockSpec((pd, d), const2),
    ]
    args = [ys, x1, gates, slots, p_rows, g_ple.reshape(1, d), w_gate.astype(BF16),
            b_gate.reshape(1, d), w_proj.astype(BF16)]
    n_exp = plan["pad8"].shape[0]
    scratch = [
        pltpu.VMEM((2, _local_rows(tc, n_exp), d), F32),
        pltpu.SemaphoreType.DMA((2,)),
    ]
    if final:
        in_specs.append(pl.BlockSpec((1, d), const2))
        args.append(g_final.reshape(1, d))
        out_specs = pl.BlockSpec((1, tc, d), tile3)
        out_shape = jax.ShapeDtypeStruct((b, s, d), F32)
    else:
        g_kv, w_k, w_v, w_f, b_f, g_q, w_q = kv
        h = w_f.shape[1]
        dh = d // h
        hs = h * HEAD_SLOT
        npc = N_DECAY_PIECES

        def slotted(w):
            w3 = jnp.pad(w.reshape(d, h, dh), ((0, 0), (0, 0), (0, HEAD_SLOT - dh)))
            return w3.reshape(d, hs)

        hc = npc * h
        selk = np.zeros((hc, hs), np.float32)
        selqt = np.zeros((hs, hc), np.float32)
        onesk = np.zeros((1, hs), np.float32)
        onesq = np.zeros((hs, 1), np.float32)
        for hd in range(h):
            for jp in range(npc):
                selqt[hd * HEAD_SLOT + dh + jp, jp * h + hd] = 1.0
                onesk[0, hd * HEAD_SLOT + dh + jp] = 1.0
                selk[jp * h + hd, hd * HEAD_SLOT + dh + npc + jp] = 1.0
                onesq[hd * HEAD_SLOT + dh + npc + jp, 0] = 1.0
        in_specs += [
            pl.BlockSpec((1, d), const2),
            pl.BlockSpec((d, hs), const2),
            pl.BlockSpec((d, d), const2),
            pl.BlockSpec((d, hc), const2),
            pl.BlockSpec((1, hc), const2),
            pl.BlockSpec((1, d), const2),
            pl.BlockSpec((hs, d), const2),
            pl.BlockSpec((hc, hs), const2),
            pl.BlockSpec((hs, hc), const2),
            pl.BlockSpec((1, hs), const2),
            pl.BlockSpec((hs, 1), const2),
            pl.BlockSpec((hc, hc), const2),
            pl.BlockSpec((tc, tc), const2),
        ]
        qscale = LOG2E / math.sqrt(dh)
        args += [g_kv.reshape(1, d), slotted(w_k).astype(BF16), w_v.T.astype(BF16),
                 jnp.tile(w_f, (1, npc)), jnp.tile(b_f, npc).reshape(1, hc), g_q.reshape(1, d),
                 slotted(w_q * qscale).T.astype(BF16),
                 jnp.asarray(selk, BF16), jnp.asarray(selqt, BF16), jnp.asarray(onesk),
                 jnp.asarray(onesq), jnp.eye(hc, dtype=F32), _lower_tri(tc, strict=False)]
        out_specs = [
            pl.BlockSpec((1, tc, d), tile3),
            pl.BlockSpec((1, h, HEAD_SLOT, tc), lambda bi, i, *_: (bi, 0, 0, i)),
            pl.BlockSpec((1, h, tc, HEAD_SLOT), lambda bi, i, *_: (bi, 0, i, 0)),
            pl.BlockSpec((1, d, tc), lambda bi, i, *_: (bi, 0, i)),
        ]
        out_shape = [
            jax.ShapeDtypeStruct((b, s, d), F32),
            jax.ShapeDtypeStruct((b, h, HEAD_SLOT, s), BF16),
            jax.ShapeDtypeStruct((b, h, s, HEAD_SLOT), BF16),
            jax.ShapeDtypeStruct((b, d, s), BF16),
        ]
        scratch += [pltpu.VMEM((1, hc), F32)]
    return pl.pallas_call(
        functools.partial(_combine_kernel, final=final, n_exp=n_exp),
        grid_spec=pltpu.PrefetchScalarGridSpec(
            num_scalar_prefetch=4,
            grid=(b, nt),
            in_specs=in_specs,
            out_specs=out_specs,
            scratch_shapes=scratch,
        ),
        out_shape=out_shape,
        compiler_params=_cparams(2),
        name="combine_final" if final else "combine_kv",
    )(plan["run8"], plan["base"], plan["off"], plan["tot8"], *args)


def _attn_kernel(qi_ref, kj_ref, qt_ref, kp_ref, vt_ref, x_ref, wo_ref,
                 gffn_ref, rw_ref, rb_ref, tri_ref,
                 x4_ref, h2_ref, gate_ref, slot_ref, cnt_ref,
                 m_ref, l_ref, acc_ref, st_ref):
    pr = pl.program_id(1)
    qi = qi_ref[pr]
    kj = kj_ref[pr]
    n_heads = qt_ref.shape[1]
    tq = qt_ref.shape[3]
    tk = kp_ref.shape[2]
    dh = vt_ref.shape[1] // n_heads
    n_sub = cnt_ref.shape[0]
    tr = tq // n_sub

    @pl.when(kj == 0)
    def _():
        m_ref[...] = jnp.full_like(m_ref, NEG_INF)
        l_ref[...] = jnp.zeros_like(l_ref)
        acc_ref[...] = jnp.zeros_like(acc_ref)

    def logits(hd):
        st_ref[hd % 2] = jnp.dot(kp_ref[0, hd], qt_ref[0, hd], preferred_element_type=F32)

    def head_step(hd, masked):
        st = st_ref[hd % 2]
        if masked:
            krow = lax.broadcasted_iota(jnp.int32, (tk, tq), 0)
            qcol = lax.broadcasted_iota(jnp.int32, (tk, tq), 1)
            st = jnp.where(krow <= qcol, st, NEG_INF)
        m_old = m_ref[hd]
        m_new = jnp.maximum(m_old, jnp.max(st, axis=0, keepdims=True))
        a = jnp.exp2(m_old - m_new)
        pt = jnp.exp2(st - m_new)
        l_ref[hd] = a * l_ref[hd] + jnp.sum(pt, axis=0, keepdims=True)
        m_ref[hd] = m_new
        pv = jnp.dot(vt_ref[0, hd * dh:(hd + 1) * dh, :], pt.astype(BF16),
                     preferred_element_type=F32)
        acc_ref[hd * dh:(hd + 1) * dh, :] = a * acc_ref[hd * dh:(hd + 1) * dh, :] + pv

    def all_heads(masked):
        logits(0)
        for hd in range(n_heads):
            if hd + 1 < n_heads:
                logits(hd + 1)
            head_step(hd, masked)

    @pl.when(kj < qi)
    def _():
        all_heads(False)

    @pl.when(kj == qi)
    def _():
        all_heads(True)
        parts = [acc_ref[hd * dh:(hd + 1) * dh, :] * (1.0 / l_ref[hd]) for hd in range(n_heads)]
        o = jnp.concatenate(parts, axis=0).T.astype(BF16)
        x4 = x_ref[0] + jnp.dot(o, wo_ref[...], preferred_element_type=F32)
        x4_ref[0] = x4
        h2 = _rms(x4, gffn_ref[...])
        h2_ref[0] = h2.astype(BF16)
        for sb in range(n_sub):
            rs = slice(sb * tr, (sb + 1) * tr)
            _route(h2[rs], rw_ref, rb_ref, tri_ref, gate_ref.at[rs], slot_ref.at[rs],
                   cnt_ref.at[sb])


def _attn(qt, kp, vt, x3, w_o, g_ffn, router_w, router_b, tq, tr):
    b, s, d = x3.shape
    h = qt.shape[1]
    e = router_w.shape[1]
    n = b * s
    nq = s // tq
    tk = tq
    n_sub = tq // tr
    pairs = [(i, j) for i in range(nq) for j in range(i + 1)]
    qi_arr = jnp.asarray(np.array([p[0] for p in pairs], np.int32))
    kj_arr = jnp.asarray(np.array([p[1] for p in pairs], np.int32))
    qtile = lambda bi, pr, qa, ka: (bi, qa[pr], 0)
    const2 = lambda bi, pr, qa, ka: (0, 0)
    tok = lambda bi, pr, qa, ka: (bi * nq + qa[pr], 0)
    return pl.pallas_call(
        _attn_kernel,
        grid_spec=pltpu.PrefetchScalarGridSpec(
            num_scalar_prefetch=2,
            grid=(b, len(pairs)),
            in_specs=[
                pl.BlockSpec((1, h, HEAD_SLOT, tq), lambda bi, pr, qa, ka: (bi, 0, 0, qa[pr])),
                pl.BlockSpec((1, h, tk, HEAD_SLOT), lambda bi, pr, qa, ka: (bi, 0, ka[pr], 0)),
                pl.BlockSpec((1, d, tk), lambda bi, pr, qa, ka: (bi, 0, ka[pr])),
                pl.BlockSpec((1, tq, d), qtile),
                pl.BlockSpec((d, d), const2),
                pl.BlockSpec((1, d), const2),
                pl.BlockSpec((d, e), const2),
                pl.BlockSpec((1, e), const2),
                pl.BlockSpec((tr, tr), const2),
            ],
            out_specs=[
                pl.BlockSpec((1, tq, d), qtile),
                pl.BlockSpec((1, tq, d), qtile),
                pl.BlockSpec((tq, TOP_K), tok),
                pl.BlockSpec((tq, TOP_K), tok),
                pl.BlockSpec((n_sub, 1, e), lambda bi, pr, qa, ka: (bi * nq + qa[pr], 0, 0)),
            ],
            scratch_shapes=[
                pltpu.VMEM((h, 1, tq), F32),
                pltpu.VMEM((h, 1, tq), F32),
                pltpu.VMEM((d, tq), F32),
                pltpu.VMEM((2, tk, tq), F32),
            ],
        ),
        out_shape=[
            jax.ShapeDtypeStruct((b, s, d), F32),
            jax.ShapeDtypeStruct((b, s, d), BF16),
            jax.ShapeDtypeStruct((n, TOP_K), F32),
            jax.ShapeDtypeStruct((n, TOP_K), F32),
            jax.ShapeDtypeStruct((n // tr, 1, e), jnp.int32),
        ],
        compiler_params=_cparams(2),
        name="attn",
    )(qi_arr, kj_arr, qt, kp, vt, x3, w_o.astype(BF16), g_ffn.reshape(1, d),
      router_w, router_b.reshape(1, e), _lower_tri(tr, strict=True))


def _tiles(b, s):
    n = b * s
    tr = min(256, s)
    tm = min(512, n)
    tq = min(512, s)
    return tr, tm, tq


def _moe(h2, slots, cnt_tiles, w_gu, b_gu, w_down, b_down, layer, tr, tm):
    n, d = h2.shape
    e = w_gu.shape[1]
    num_tiles = _local_rows(tr, e) * (n // tr) // tm + e
    plan = _plan(cnt_tiles, tm, num_tiles)
    xs = _dispatch(h2, slots, plan, num_tiles * tm, tr, tm)
    ys = _experts(xs, plan, w_gu, b_gu[layer], w_down, b_down[layer], layer, tm)
    return plan, ys


def kernel(x, p, norm_mix, norm_ffn, norm_ple, norm_final, pool_w, pool_scale, kv_norm, w_k, w_v,
           w_fgate, b_fgate, w_q, w_o, router_w, router_b, exp_w_gu, exp_b_gu, exp_w_down,
           exp_b_down, ple_w_gate, ple_b_gate, ple_w_proj):
    b, s, d = x.shape
    n = b * s
    tr, tm, tq = _tiles(b, s)

    x1, h2, gates, slots, cnt_tiles = _mix0(
        x, norm_mix[0], pool_w[0], pool_scale[0], norm_ffn[0], router_w[0], router_b[0], tr)
    plan, ys = _moe(h2.reshape(n, d), slots, cnt_tiles,
                    exp_w_gu, exp_b_gu, exp_w_down, exp_b_down, 0, tr, tm)
    x3, qt, kp, vt = _combine(
        plan, ys, x1, gates, slots, p, 0, norm_ple[0], ple_w_gate[0], ple_b_gate[0],
        ple_w_proj[0], tr, kv=(kv_norm, w_k, w_v, w_fgate, b_fgate, norm_mix[1], w_q[0]))

    x4, h2, gates, slots, cnt_tiles = _attn(
        qt, kp, vt, x3, w_o[0], norm_ffn[1], router_w[1], router_b[1], tq, tr)
    plan, ys = _moe(h2.reshape(n, d), slots, cnt_tiles,
                    exp_w_gu, exp_b_gu, exp_w_down, exp_b_down, 1, tr, tm)
    return _combine(plan, ys, x4, gates, slots, p, 1, norm_ple[1], ple_w_gate[1], ple_b_gate[1],
                    ple_w_proj[1], tr, g_final=norm_final)
```

```python
import functools
import math

import numpy as np
import jax
import jax.numpy as jnp
from jax import lax
from jax.experimental import pallas as pl
from jax.experimental.pallas import tpu as pltpu

TOP_K = 4
POOL_WINDOWS = (2, 4, 8, 16)
POOL_HALO = 16
SWIGLU_LIMIT = 7.0
SWIGLU_ALPHA = 1.702
RMS_EPS = 1e-6
NEG_INF = -1e30
LOG2E = math.log2(math.e)
HEAD_SLOT = 128
N_DECAY_PIECES = 3
SUBLANES = 8
LONG_RUN_BIT = 3
N_KEY_STATS = 3
SKIP_LOG2_MARGIN = 160.0
NORM_SLACK = 1.05
STEP_ACTIVE, STEP_FIRST = 1, 2

F32 = jnp.float32
BF16 = jnp.bfloat16
NT_DIMS = (((1,), (1,)), ((), ()))

V7X_VMEM_BYTES = 64 * 1024 * 1024
VMEM_LIMIT = V7X_VMEM_BYTES - 8 * 1024 * 1024


def _cparams(n_axes):
    return pltpu.CompilerParams(
        dimension_semantics=("arbitrary",) * n_axes,
        vmem_limit_bytes=VMEM_LIMIT,
    )


def _rms(v, g):
    return v * lax.rsqrt(jnp.mean(v * v, axis=-1, keepdims=True) + RMS_EPS) * g


def _split3(a):
    hi = a.astype(BF16)
    r1 = a - hi.astype(F32)
    mid = r1.astype(BF16)
    lo = (r1 - mid.astype(F32)).astype(BF16)
    return hi, mid, lo


def _mm(a, b, dims):
    if dims is None:
        return jnp.dot(a, b, preferred_element_type=F32)
    return lax.dot_general(a, b, dims, preferred_element_type=F32)


def _dot_x3(a, b, dims=None):
    a_hi = a.astype(BF16)
    a_lo = (a - a_hi.astype(F32)).astype(BF16)
    b_hi = b.astype(BF16)
    b_lo = (b - b_hi.astype(F32)).astype(BF16)
    return _mm(a_hi, b_hi, dims) + _mm(a_hi, b_lo, dims) + _mm(a_lo, b_hi, dims)


def _dot_sel(sel, x, dims=None, sel_first=True):
    sel = sel.astype(BF16)
    out = None
    for piece in _split3(x):
        term = _mm(sel, piece, dims) if sel_first else _mm(piece, sel, dims)
        out = term if out is None else out + term
    return out


def _lower_tri(t, strict):
    r = np.arange(t)
    keep = r[None, :] < r[:, None] if strict else r[None, :] <= r[:, None]
    return jnp.asarray(keep, BF16)


def _route(h2, rw_ref, rb_ref, tri_ref, gate_ref, slot_ref, cnt_ref):
    t = h2.shape[0]
    e = rw_ref.shape[1]
    logits = _dot_x3(h2, rw_ref[...]) + rb_ref[...]
    lane = lax.broadcasted_iota(jnp.int32, (t, e), 1).astype(F32)
    work = logits
    sels, vals = [], []
    for _ in range(TOP_K):
        m = jnp.max(work, axis=-1, keepdims=True)
        idx = jnp.min(jnp.where(work == m, lane, float(e)), axis=-1, keepdims=True)
        sel = lane == idx
        work = jnp.where(sel, -jnp.inf, work)
        sels.append(sel)
        vals.append(m)
    exps = [jnp.exp(v - vals[0]) for v in vals]
    denom = exps[0] + exps[1] + exps[2] + exps[3]
    onehot = sels[0].astype(F32) + sels[1].astype(F32) + sels[2].astype(F32) + sels[3].astype(F32)
    excl = jnp.dot(tri_ref[...], onehot.astype(BF16), preferred_element_type=F32)
    cnt = jnp.sum(onehot, axis=0, keepdims=True)
    erow = lax.broadcasted_iota(jnp.int32, (e, e), 0)
    ecol = lax.broadcasted_iota(jnp.int32, (e, e), 1)
    run8 = jnp.floor((cnt + (SUBLANES - 1.0)) * (1.0 / SUBLANES))
    off = SUBLANES * jnp.dot(jnp.broadcast_to(run8, (SUBLANES, e)).astype(BF16),
                             (erow < ecol).astype(BF16), preferred_element_type=F32)[0:1, :]
    tot = excl + off
    lane_k = lax.broadcasted_iota(jnp.int32, (t, TOP_K), 1)
    gate_out = jnp.zeros((t, TOP_K), F32)
    slot_out = jnp.zeros((t, TOP_K), F32)
    for k in range(TOP_K):
        slot_k = jnp.sum(jnp.where(sels[k], tot, 0.0), axis=-1, keepdims=True)
        gate_out = jnp.where(lane_k == k, exps[k] / denom, gate_out)
        slot_out = jnp.where(lane_k == k, slot_k, slot_out)
    gate_ref[...] = gate_out
    slot_ref[...] = slot_out
    cnt_ref[...] = cnt.astype(jnp.int32)


def _mix0_kernel(x_ref, xh_ref, gmix_ref, pw_ref, ps_ref, gffn_ref, rw_ref, rb_ref, tri_ref,
                 x1_ref, h2_ref, gate_ref, slot_ref, cnt_ref):
    i = pl.program_id(1)
    ts = x_ref.shape[1]
    d = x_ref.shape[2]
    cg = d // len(POOL_WINDOWS)

    xt = x_ref[0]
    g = gmix_ref[...]
    h = _rms(xt, g)
    hh = jnp.where(i > 0, _rms(xh_ref[0], g), 0.0)
    hcat = jnp.concatenate([hh, h], axis=0)
    pos = i * ts + lax.broadcasted_iota(jnp.int32, (ts, 1), 0)
    outs = []
    for gi, w in enumerate(POOL_WINDOWS):
        cur = hcat[:, gi * cg:(gi + 1) * cg]
        span = 1
        while span < w:
            cur = cur + pltpu.roll(cur, span, 0)
            span *= 2
        wsum = cur[POOL_HALO:, :]
        count = jnp.minimum(pos + 1, w).astype(F32)
        diff = wsum / count - h[:, gi * cg:(gi + 1) * cg]
        outs.append(jnp.dot(diff.astype(BF16), pw_ref[gi], preferred_element_type=F32))
    mixed = jnp.concatenate(outs, axis=-1) * ps_ref[...]
    x1 = xt + mixed
    x1_ref[0] = x1
    h2 = _rms(x1, gffn_ref[...])
    h2_ref[0] = h2.astype(BF16)
    _route(h2, rw_ref, rb_ref, tri_ref, gate_ref, slot_ref, cnt_ref.at[0])


def _mix0(x, g_mix, pool_w, pool_scale, g_ffn, router_w, router_b, ts):
    b, s, d = x.shape
    e = router_w.shape[1]
    n = b * s
    nt = s // ts
    hb = ts // POOL_HALO
    ng = len(POOL_WINDOWS)
    cg = d // ng
    const2 = lambda bi, i: (0, 0)
    tok = lambda bi, i: (bi * nt + i, 0)
    return pl.pallas_call(
        _mix0_kernel,
        grid=(b, nt),
        in_specs=[
            pl.BlockSpec((1, ts, d), lambda bi, i: (bi, i, 0)),
            pl.BlockSpec((1, POOL_HALO, d), lambda bi, i: (bi, jnp.maximum(i * hb - 1, 0), 0)),
            pl.BlockSpec((1, d), const2),
            pl.BlockSpec((ng, cg, cg), lambda bi, i: (0, 0, 0)),
            pl.BlockSpec((1, d), const2),
            pl.BlockSpec((1, d), const2),
            pl.BlockSpec((d, e), const2),
            pl.BlockSpec((1, e), const2),
            pl.BlockSpec((ts, ts), const2),
        ],
        out_specs=[
            pl.BlockSpec((1, ts, d), lambda bi, i: (bi, i, 0)),
            pl.BlockSpec((1, ts, d), lambda bi, i: (bi, i, 0)),
            pl.BlockSpec((ts, TOP_K), tok),
            pl.BlockSpec((ts, TOP_K), tok),
            pl.BlockSpec((1, 1, e), lambda bi, i: (bi * nt + i, 0, 0)),
        ],
        out_shape=[
            jax.ShapeDtypeStruct((b, s, d), F32),
            jax.ShapeDtypeStruct((b, s, d), BF16),
            jax.ShapeDtypeStruct((n, TOP_K), F32),
            jax.ShapeDtypeStruct((n, TOP_K), F32),
            jax.ShapeDtypeStruct((n // ts, 1, e), jnp.int32),
        ],
        compiler_params=_cparams(2),
        name="mix0",
    )(x, x, g_mix.reshape(1, d), pool_w.astype(BF16), pool_scale.reshape(1, d),
      g_ffn.reshape(1, d), router_w, router_b.reshape(1, e), _lower_tri(ts, strict=True))


def _plan(cnt_tiles, tm, num_tiles):
    nt, _, e = cnt_tiles.shape
    tm8 = tm // SUBLANES
    run8 = (cnt_tiles.reshape(nt, e) + SUBLANES - 1) // SUBLANES
    total8 = jnp.sum(run8, axis=0)
    tiles_per = (total8 + tm8 - 1) // tm8
    tile_end = jnp.cumsum(tiles_per)
    tile_start = tile_end - tiles_per
    base = tile_start[None, :] * tm + SUBLANES * (jnp.cumsum(run8, axis=0) - run8)
    off = SUBLANES * (jnp.cumsum(run8, axis=1) - run8)
    pad_start = tile_start * tm + SUBLANES * total8
    pad8 = tiles_per * tm8 - total8
    n_used = tile_end[-1]
    t = jnp.arange(num_tiles, dtype=jnp.int32)
    tile_expert = jnp.sum((t[:, None] >= tile_end[None, :]).astype(jnp.int32), axis=1)
    tile_expert = jnp.minimum(tile_expert, e - 1)
    pad_total8 = jnp.sum(pad8)
    zero_full = pad_total8 // tm8 + (num_tiles - n_used)
    misc = jnp.stack([n_used, zero_full, pad_total8 % tm8])
    i32 = lambda a: a.reshape(-1).astype(jnp.int32)
    return dict(run8=i32(run8), base=i32(base), off=i32(off), tot8=i32(jnp.sum(run8, axis=1)),
                pad_start=i32(pad_start), pad8=i32(pad8), misc=i32(misc),
                tile_expert=i32(tile_expert), n_used=i32(n_used))


def _copy_run(n8, src, src_row, dst, dst_row, sem, max8):
    def copy_bit(bit):
        size = SUBLANES << bit

        @pl.when((n8 & (1 << bit)) != 0)
        def _():
            done = ((n8 >> (bit + 1)) << (bit + 1)) * SUBLANES
            pltpu.make_async_copy(
                src.at[pl.ds(pl.multiple_of(src_row + done, SUBLANES), size), :],
                dst.at[pl.ds(pl.multiple_of(dst_row + done, SUBLANES), size), :], sem).start()

    bits = list(reversed(range(max8.bit_length())))
    long_bits = [bit for bit in bits if bit >= LONG_RUN_BIT]
    if long_bits:
        @pl.when(n8 >= (1 << LONG_RUN_BIT))
        def _():
            for bit in long_bits:
                copy_bit(bit)
    for bit in bits:
        if bit < LONG_RUN_BIT:
            copy_bit(bit)


def _wait_run(n8, src, dst, sem, max8):
    for bit in reversed(range(max8.bit_length())):
        size = SUBLANES << bit

        @pl.when((n8 & (1 << bit)) != 0)
        def _():
            pltpu.make_async_copy(src.at[pl.ds(0, size), :], dst.at[pl.ds(0, size), :], sem).wait()


def _dispatch_kernel(run8_s, base_s, off_s, tot8_s, pstart_s, pad8_s, misc_s,
                     h_ref, slot_ref, eye_ref, xs_ref, xbuf, zbuf, sem, zsem):
    t = pl.program_id(0)
    nt = pl.num_programs(0)
    td = h_ref.shape[0]
    rows = xbuf.shape[1]
    tm = zbuf.shape[0]
    n_exp = pad8_s.shape[0]
    num_tiles = xs_ref.shape[0] // tm
    cur = t % 2

    def wait_step(step, s):
        _wait_run(tot8_s[step], xbuf.at[s], xs_ref, sem.at[s], rows // SUBLANES)

    @pl.when(t >= 2)
    def _():
        wait_step(t - 2, cur)

    srow = _dot_sel(eye_ref[...], slot_ref[...], NT_DIMS)
    j = lax.broadcasted_iota(jnp.int32, (rows, td), 0).astype(F32)
    pick = j == srow[0:1, :]
    for k in range(1, TOP_K):
        pick = pick | (j == srow[k:k + 1, :])
    xbuf[cur] = jnp.dot(pick.astype(BF16), h_ref[...], preferred_element_type=F32)

    for ex in range(n_exp):
        q = t * n_exp + ex
        _copy_run(run8_s[q], xbuf.at[cur], off_s[q], xs_ref, base_s[q], sem.at[cur],
                  td // SUBLANES)

    @pl.when(t == nt - 1)
    def _():
        zbuf[...] = jnp.zeros_like(zbuf)
        tm8 = tm // SUBLANES
        for ex in range(n_exp):
            _copy_run(pad8_s[ex], zbuf, 0, xs_ref, pstart_s[ex], zsem, tm8 - 1)

        def zero_tile(tile, c):
            pltpu.make_async_copy(zbuf, xs_ref.at[pl.ds(pl.multiple_of(tile * tm, tm), tm), :],
                                  zsem).start()
            return c

        lax.fori_loop(misc_s[0], num_tiles, zero_tile, 0)

        def wait_tile(_, c):
            pltpu.make_async_copy(zbuf, xs_ref.at[pl.ds(0, tm), :], zsem).wait()
            return c

        lax.fori_loop(0, misc_s[1], wait_tile, 0)
        _wait_run(misc_s[2], zbuf, xs_ref, zsem, tm8 - 1)
        wait_step(t, cur)

        @pl.when(nt >= 2)
        def _():
            wait_step(t - 1, 1 - cur)


def _local_rows(t, n_exp):
    return t * TOP_K + SUBLANES * n_exp


def _dispatch(h2, slots, plan, rows_padded, td, tm):
    n, d = h2.shape
    n_exp = plan["pad8"].shape[0]
    return pl.pallas_call(
        _dispatch_kernel,
        grid_spec=pltpu.PrefetchScalarGridSpec(
            num_scalar_prefetch=7,
            grid=(n // td,),
            in_specs=[
                pl.BlockSpec((td, d), lambda t, *_: (t, 0)),
                pl.BlockSpec((td, TOP_K), lambda t, *_: (t, 0)),
                pl.BlockSpec((SUBLANES, TOP_K), lambda t, *_: (0, 0)),
            ],
            out_specs=pl.BlockSpec(memory_space=pl.ANY),
            scratch_shapes=[
                pltpu.VMEM((2, _local_rows(td, n_exp), d), F32),
                pltpu.VMEM((tm, d), F32),
                pltpu.SemaphoreType.DMA((2,)),
                pltpu.SemaphoreType.DMA(()),
            ],
        ),
        out_shape=jax.ShapeDtypeStruct((rows_padded, d), F32),
        compiler_params=_cparams(1),
        name="dispatch",
    )(plan["run8"], plan["base"], plan["off"], plan["tot8"], plan["pad_start"], plan["pad8"],
      plan["misc"], h2, slots, jnp.eye(SUBLANES, TOP_K, dtype=F32))


def _experts_kernel(texp_ref, nused_ref, xs_ref, wgu_ref, bgu_ref, wd_ref, bd_ref, ys_ref,
                    wgu_bf, wd_bf):
    t = pl.program_id(0)
    f = wd_ref.shape[2]

    @pl.when((t == 0) | (texp_ref[t] != texp_ref[jnp.maximum(t - 1, 0)]))
    def _():
        wgu_bf[...] = wgu_ref[0, 0].astype(BF16)
        wd_bf[...] = wd_ref[0, 0].astype(BF16)

    @pl.when(t < nused_ref[0])
    def _():
        xb = xs_ref[...].astype(BF16)
        gu = jnp.dot(xb, wgu_bf[...], preferred_element_type=F32) + bgu_ref[0]
        gate = jnp.minimum(gu[:, :f], SWIGLU_LIMIT)
        up = jnp.clip(gu[:, f:], -SWIGLU_LIMIT, SWIGLU_LIMIT)
        y = (up + 1.0) * (gate * jax.nn.sigmoid(SWIGLU_ALPHA * gate))
        ys_ref[...] = jnp.dot(y.astype(BF16), wd_bf[...], preferred_element_type=F32) + bd_ref[0]

    @pl.when(t >= nused_ref[0])
    def _():
        ys_ref[...] = jnp.zeros_like(ys_ref)


def _experts(xs, plan, w_gu, b_gu, w_down, b_down, layer, tm):
    rows, d = xs.shape
    _, e, _, f2 = w_gu.shape
    f = f2 // 2
    num_tiles = rows // tm
    expert = lambda t, te, nu: (te[t], 0, 0)
    layer_expert = lambda t, te, nu: (layer, te[t], 0, 0)
    return pl.pallas_call(
        _experts_kernel,
        grid_spec=pltpu.PrefetchScalarGridSpec(
            num_scalar_prefetch=2,
            grid=(num_tiles,),
            in_specs=[
                pl.BlockSpec((tm, d),
                             lambda t, te, nu: (jnp.maximum(jnp.minimum(t, nu[0] - 1), 0), 0)),
                pl.BlockSpec((1, 1, d, f2), layer_expert),
                pl.BlockSpec((1, 1, f2), expert),
                pl.BlockSpec((1, 1, f, d), layer_expert),
                pl.BlockSpec((1, 1, d), expert),
            ],
            out_specs=pl.BlockSpec((tm, d), lambda t, te, nu: (t, 0)),
            scratch_shapes=[pltpu.VMEM((d, f2), BF16), pltpu.VMEM((f, d), BF16)],
        ),
        out_shape=jax.ShapeDtypeStruct((rows, d), F32),
        compiler_params=_cparams(1),
        name="experts",
    )(plan["tile_expert"], plan["n_used"], xs, w_gu, b_gu.reshape(e, 1, f2), w_down,
      b_down.reshape(e, 1, d))


def _combine_kernel(*refs, final, n_exp):
    if final:
        (run8_s, base_s, off_s, tot8_s, ys_hbm, x_ref, gates_ref, slot_ref, p_ref, gple_ref,
         wg_ref, bg_ref, wp_ref, gfin_ref, out_ref, ybuf, gsem) = refs
    else:
        (run8_s, base_s, off_s, tot8_s, ys_hbm, x_ref, gates_ref, slot_ref, p_ref, gple_ref,
         wg_ref, bg_ref, wp_ref,
         gkv_ref, wkp_ref, wvt_ref, wf_ref, bf_ref, gq_ref, wqt_ref,
         selk_ref, selqt_ref, onesk_ref, onesq_ref, eye_ref, tri_ref, hsum_ref, hsumt_ref,
         x3_ref, qt_ref, kp_ref, vt_ref, krow_ref, qcol_ref, ybuf, gsem, carry) = refs
    bi = pl.program_id(0)
    i = pl.program_id(1)
    nt = pl.num_programs(1)
    step = bi * nt + i
    nsteps = pl.num_programs(0) * nt
    tc = x_ref.shape[1]
    rows = ybuf.shape[1]
    cur = step % 2

    def fetch(s, slot):
        for ex in range(n_exp):
            q = s * n_exp + ex
            _copy_run(run8_s[q], ys_hbm, base_s[q], ybuf.at[slot], off_s[q], gsem.at[slot],
                      tc // SUBLANES)

    @pl.when(step == 0)
    def _():
        ybuf[...] = jnp.zeros_like(ybuf)
        fetch(step, cur)

    @pl.when(step + 1 < nsteps)
    def _():
        fetch(step + 1, 1 - cur)

    proj = jnp.dot(p_ref[0].astype(BF16), wp_ref[...], preferred_element_type=F32)
    x1 = x_ref[0]
    gates = gates_ref[...]
    slots = slot_ref[...]
    j = lax.broadcasted_iota(jnp.int32, (tc, rows), 1).astype(F32)
    gmat = jnp.where(j == slots[:, 0:1], gates[:, 0:1], 0.0)
    for k in range(1, TOP_K):
        gmat = gmat + jnp.where(j == slots[:, k:k + 1], gates[:, k:k + 1], 0.0)
    g_hi = gmat.astype(BF16)
    g_lo = (gmat - g_hi.astype(F32)).astype(BF16)

    _wait_run(tot8_s[step], ys_hbm, ybuf.at[cur], gsem.at[cur], rows // SUBLANES)
    yb = ybuf[cur].astype(BF16)
    moe = (jnp.dot(g_hi, yb, preferred_element_type=F32)
           + jnp.dot(g_lo, yb, preferred_element_type=F32))
    x2 = x1 + moe
    hn = _rms(x2, gple_ref[...])
    pg = jax.nn.sigmoid(jnp.dot(hn.astype(BF16), wg_ref[...], preferred_element_type=F32) + bg_ref[...])
    x3 = x2 + pg * proj

    if final:
        out_ref[0] = _rms(x3, gfin_ref[...])
        return

    x3_ref[0] = x3
    hk = _rms(x3, gkv_ref[...])
    hkb = hk.astype(BF16)
    n_heads = kp_ref.shape[1]
    vt_ref[0] = lax.dot_general(wvt_ref[...], hkb, NT_DIMS, preferred_element_type=F32).astype(BF16)

    z = _dot_x3(hk, wf_ref[...]) + bf_ref[...]
    lf = jax.nn.log_sigmoid(z) * LOG2E

    @pl.when(i == 0)
    def _():
        carry[...] = jnp.zeros_like(carry)

    a_c = _dot_sel(tri_ref[...], lf) + carry[...]
    carry[...] = a_c[tc - 1:tc, :]
    a_r = _dot_sel(eye_ref[...], a_c, NT_DIMS)

    def piece_of_copy(a, copy_index):
        out = None
        for jp, pc in enumerate(_split3(a)):
            out = pc if out is None else jnp.where(copy_index == jp, pc, out)
        return out

    pk = piece_of_copy(-a_c, lax.broadcasted_iota(jnp.int32, a_c.shape, 1) // n_heads)
    k_main = jnp.dot(hkb, wkp_ref[...], preferred_element_type=F32)
    kp = k_main + onesk_ref[...] + jnp.dot(pk, selk_ref[...], preferred_element_type=F32)
    hq = _rms(x3, gq_ref[...])
    pq = piece_of_copy(a_r, lax.broadcasted_iota(jnp.int32, a_r.shape, 0) // n_heads)
    q_main = lax.dot_general(wqt_ref[...], hq.astype(BF16), NT_DIMS, preferred_element_type=F32)
    qt = q_main + onesq_ref[...] + jnp.dot(selqt_ref[...], pq, preferred_element_type=F32)

    kb = k_main.astype(BF16).astype(F32)
    qb = q_main.astype(BF16).astype(F32)
    ksq = jnp.dot((kb * kb).astype(BF16), hsum_ref[...], preferred_element_type=F32)
    qsq = jnp.dot(hsumt_ref[...], (qb * qb).astype(BF16), preferred_element_type=F32)
    krow_ref[0, 0:1, :] = jnp.max(ksq, axis=0, keepdims=True)
    krow_ref[0, 1:2, :] = jnp.max(a_c[:, :n_heads], axis=0, keepdims=True)
    krow_ref[0, 2:3, :] = jnp.min(a_c[:, :n_heads], axis=0, keepdims=True)
    qcol_ref[0] = jnp.max(qsq, axis=1, keepdims=True)
    for hd in range(n_heads):
        kp_ref[0, hd] = kp[:, hd * HEAD_SLOT:(hd + 1) * HEAD_SLOT].astype(BF16)
        qt_ref[0, hd] = qt[hd * HEAD_SLOT:(hd + 1) * HEAD_SLOT, :].astype(BF16)


def _combine(plan, ys, x1, gates, slots, p_all, layer, g_ple, w_gate, b_gate, w_proj, tc, *,
             g_final=None, kv=None):
    b, s, d = x1.shape
    pd = p_all.shape[-1]
    p_rows = p_all.reshape(-1, s, pd)
    nt = s // tc
    final = g_final is not None
    const2 = lambda bi, i, *_: (0, 0)
    tile3 = lambda bi, i, *_: (bi, i, 0)
    tok = lambda bi, i, *_: (bi * nt + i, 0)
    in_specs = [
        pl.BlockSpec(memory_space=pl.ANY),
        pl.BlockSpec((1, tc, d), tile3),
        pl.BlockSpec((tc, TOP_K), tok),
        pl.BlockSpec((tc, TOP_K), tok),
        pl.BlockSpec((1, tc, pd), lambda bi, i, *_: (layer * b + bi, i, 0)),
        pl.BlockSpec((1, d), const2),
        pl.BlockSpec((d, d), const2),
        pl.BlockSpec((1, d), const2),
        pl.BlockSpec((pd, d), const2),
    ]
    args = [ys, x1, gates, slots, p_rows, g_ple.reshape(1, d), w_gate.astype(BF16),
            b_gate.reshape(1, d), w_proj.astype(BF16)]
    n_exp = plan["pad8"].shape[0]
    scratch = [
        pltpu.VMEM((2, _local_rows(tc, n_exp), d), F32),
        pltpu.SemaphoreType.DMA((2,)),
    ]
    if final:
        in_specs.append(pl.BlockSpec((1, d), const2))
        args.append(g_final.reshape(1, d))
        out_specs = pl.BlockSpec((1, tc, d), tile3)
        out_shape = jax.ShapeDtypeStruct((b, s, d), F32)
    else:
        g_kv, w_k, w_v, w_f, b_f, g_q, w_q = kv
        h = w_f.shape[1]
        dh = d // h
        hs = h * HEAD_SLOT
        npc = N_DECAY_PIECES

        def slotted(w):
            w3 = jnp.pad(w.reshape(d, h, dh), ((0, 0), (0, 0), (0, HEAD_SLOT - dh)))
            return w3.reshape(d, hs)

        hc = npc * h
        selk = np.zeros((hc, hs), np.float32)
        selqt = np.zeros((hs, hc), np.float32)
        onesk = np.zeros((1, hs), np.float32)
        onesq = np.zeros((hs, 1), np.float32)
        hsum = np.zeros((hs, h), np.float32)
        for hd in range(h):
            hsum[hd * HEAD_SLOT:hd * HEAD_SLOT + dh, hd] = 1.0
            for jp in range(npc):
                selqt[hd * HEAD_SLOT + dh + jp, jp * h + hd] = 1.0
                onesk[0, hd * HEAD_SLOT + dh + jp] = 1.0
                selk[jp * h + hd, hd * HEAD_SLOT + dh + npc + jp] = 1.0
                onesq[hd * HEAD_SLOT + dh + npc + jp, 0] = 1.0
        in_specs += [
            pl.BlockSpec((1, d), const2),
            pl.BlockSpec((d, hs), const2),
            pl.BlockSpec((d, d), const2),
            pl.BlockSpec((d, hc), const2),
            pl.BlockSpec((1, hc), const2),
            pl.BlockSpec((1, d), const2),
            pl.BlockSpec((hs, d), const2),
            pl.BlockSpec((hc, hs), const2),
            pl.BlockSpec((hs, hc), const2),
            pl.BlockSpec((1, hs), const2),
            pl.BlockSpec((hs, 1), const2),
            pl.BlockSpec((hc, hc), const2),
            pl.BlockSpec((tc, tc), const2),
            pl.BlockSpec((hs, h), const2),
            pl.BlockSpec((h, hs), const2),
        ]
        qscale = LOG2E / math.sqrt(dh)
        args += [g_kv.reshape(1, d), slotted(w_k).astype(BF16), w_v.T.astype(BF16),
                 jnp.tile(w_f, (1, npc)), jnp.tile(b_f, npc).reshape(1, hc), g_q.reshape(1, d),
                 slotted(w_q * qscale).T.astype(BF16),
                 jnp.asarray(selk, BF16), jnp.asarray(selqt, BF16), jnp.asarray(onesk),
                 jnp.asarray(onesq), jnp.eye(hc, dtype=F32), _lower_tri(tc, strict=False),
                 jnp.asarray(hsum, BF16), jnp.asarray(hsum.T, BF16)]
        out_specs = [
            pl.BlockSpec((1, tc, d), tile3),
            pl.BlockSpec((1, h, HEAD_SLOT, tc), lambda bi, i, *_: (bi, 0, 0, i)),
            pl.BlockSpec((1, h, tc, HEAD_SLOT), lambda bi, i, *_: (bi, 0, i, 0)),
            pl.BlockSpec((1, d, tc), lambda bi, i, *_: (bi, 0, i)),
            pl.BlockSpec((1, N_KEY_STATS, h), lambda bi, i, *_: (bi * nt + i, 0, 0)),
            pl.BlockSpec((1, h, 1), lambda bi, i, *_: (bi * nt + i, 0, 0)),
        ]
        out_shape = [
            jax.ShapeDtypeStruct((b, s, d), F32),
            jax.ShapeDtypeStruct((b, h, HEAD_SLOT, s), BF16),
            jax.ShapeDtypeStruct((b, h, s, HEAD_SLOT), BF16),
            jax.ShapeDtypeStruct((b, d, s), BF16),
            jax.ShapeDtypeStruct((b * nt, N_KEY_STATS, h), F32),
            jax.ShapeDtypeStruct((b * nt, h, 1), F32),
        ]
        scratch += [pltpu.VMEM((1, hc), F32)]
    return pl.pallas_call(
        functools.partial(_combine_kernel, final=final, n_exp=n_exp),
        grid_spec=pltpu.PrefetchScalarGridSpec(
            num_scalar_prefetch=4,
            grid=(b, nt),
            in_specs=in_specs,
            out_specs=out_specs,
            scratch_shapes=scratch,
        ),
        out_shape=out_shape,
        compiler_params=_cparams(2),
        name="combine_final" if final else "combine_kv",
    )(plan["run8"], plan["base"], plan["off"], plan["tot8"], *args)


def _attn_kernel(qi_ref, kj_ref, flag_ref, qt_ref, kp_ref, vt_ref, x_ref, wo_ref,
                 gffn_ref, rw_ref, rb_ref, tri_ref,
                 x4_ref, h2_ref, gate_ref, slot_ref, cnt_ref,
                 m_ref, l_ref, acc_ref, st_ref):
    step = pl.program_id(0) * pl.num_programs(1) + pl.program_id(1)
    qi = qi_ref[step]
    kj = kj_ref[step]
    flags = flag_ref[step]
    active = (flags & STEP_ACTIVE) != 0
    n_heads = qt_ref.shape[1]
    tq = qt_ref.shape[3]
    tk = kp_ref.shape[2]
    dh = vt_ref.shape[1] // n_heads
    n_sub = cnt_ref.shape[0]
    tr = tq // n_sub

    @pl.when((flags & STEP_FIRST) != 0)
    def _():
        m_ref[...] = jnp.full_like(m_ref, NEG_INF)
        l_ref[...] = jnp.zeros_like(l_ref)
        acc_ref[...] = jnp.zeros_like(acc_ref)

    def logits(hd):
        st_ref[hd % 2] = jnp.dot(kp_ref[0, hd], qt_ref[0, hd], preferred_element_type=F32)

    def head_step(hd, masked):
        st = st_ref[hd % 2]
        if masked:
            krow = lax.broadcasted_iota(jnp.int32, (tk, tq), 0)
            qcol = lax.broadcasted_iota(jnp.int32, (tk, tq), 1)
            st = jnp.where(krow <= qcol, st, NEG_INF)
        m_old = m_ref[hd]
        m_new = jnp.maximum(m_old, jnp.max(st, axis=0, keepdims=True))
        a = jnp.exp2(m_old - m_new)
        pt = jnp.exp2(st - m_new)
        l_ref[hd] = a * l_ref[hd] + jnp.sum(pt, axis=0, keepdims=True)
        m_ref[hd] = m_new
        pv = jnp.dot(vt_ref[0, hd * dh:(hd + 1) * dh, :], pt.astype(BF16),
                     preferred_element_type=F32)
        acc_ref[hd * dh:(hd + 1) * dh, :] = a * acc_ref[hd * dh:(hd + 1) * dh, :] + pv

    def all_heads(masked):
        logits(0)
        for hd in range(n_heads):
            if hd + 1 < n_heads:
                logits(hd + 1)
            head_step(hd, masked)

    @pl.when(active & (kj < qi))
    def _():
        all_heads(False)

    @pl.when(active & (kj == qi))
    def _():
        all_heads(True)
        parts = [acc_ref[hd * dh:(hd + 1) * dh, :] * (1.0 / l_ref[hd]) for hd in range(n_heads)]
        o = jnp.concatenate(parts, axis=0).T.astype(BF16)
        x4 = x_ref[0] + jnp.dot(o, wo_ref[...], preferred_element_type=F32)
        x4_ref[0] = x4
        h2 = _rms(x4, gffn_ref[...])
        h2_ref[0] = h2.astype(BF16)
        for sb in range(n_sub):
            rs = slice(sb * tr, (sb + 1) * tr)
            _route(h2[rs], rw_ref, rb_ref, tri_ref, gate_ref.at[rs], slot_ref.at[rs],
                   cnt_ref.at[sb])


def _attn_plan(krow, qcol, b, nq, n_sub):
    h = krow.shape[-1]
    per_tile = lambda a: a.reshape(b, nq, n_sub, h)
    kk = jnp.max(per_tile(krow[:, 0, :]), axis=2)
    amax = jnp.max(per_tile(krow[:, 1, :]), axis=2)
    amin = jnp.min(per_tile(krow[:, 2, :]), axis=2)
    qq = jnp.max(per_tile(qcol[:, :, 0]), axis=2)
    upper = (NORM_SLACK * jnp.sqrt(qq[:, :, None, :] * kk[:, None, :, :])
             + amax[:, :, None, :] - amin[:, None, :, :])
    lower = -NORM_SLACK * jnp.sqrt(qq * kk)
    dead = jnp.all(upper <= lower[:, :, None, :] - SKIP_LOG2_MARGIN, axis=-1)

    pairs = [(i, j) for i in range(nq) for j in range(i + 1)]
    qi = np.array([p[0] for p in pairs], np.int32)
    kj = np.array([p[1] for p in pairs], np.int32)
    n_steps = len(pairs)
    keep = jnp.logical_or(jnp.asarray(qi == kj)[None, :], jnp.logical_not(dead[:, qi, kj]))
    order = jnp.argsort(jnp.logical_not(keep).astype(jnp.int32), axis=1, stable=True)
    count = jnp.sum(keep.astype(jnp.int32), axis=1, keepdims=True)
    pos = jnp.arange(n_steps, dtype=jnp.int32)[None, :]
    active = pos < count
    clamp = jnp.minimum(pos, count - 1)
    qi_c = jnp.take_along_axis(jnp.asarray(qi)[order], clamp, axis=1)
    kj_c = jnp.take_along_axis(jnp.asarray(kj)[order], clamp, axis=1)
    prev_qi = jnp.concatenate([jnp.full((b, 1), -1, jnp.int32), qi_c[:, :-1]], axis=1)
    first = jnp.logical_and(active, qi_c != prev_qi)
    flags = active.astype(jnp.int32) * STEP_ACTIVE + first.astype(jnp.int32) * STEP_FIRST
    i32 = lambda a: a.reshape(-1).astype(jnp.int32)
    return i32(qi_c), i32(kj_c), i32(flags), n_steps


def _attn(qt, kp, vt, krow, qcol, x3, w_o, g_ffn, router_w, router_b, tq, tr):
    b, s, d = x3.shape
    h = qt.shape[1]
    e = router_w.shape[1]
    n = b * s
    nq = s // tq
    tk = tq
    n_sub = tq // tr
    qi_arr, kj_arr, flag_arr, n_steps = _attn_plan(krow, qcol, b, nq, n_sub)
    at = lambda bi, pr: bi * n_steps + pr
    qtile = lambda bi, pr, qa, ka, fl: (bi, qa[at(bi, pr)], 0)
    const2 = lambda bi, pr, qa, ka, fl: (0, 0)
    tok = lambda bi, pr, qa, ka, fl: (bi * nq + qa[at(bi, pr)], 0)
    return pl.pallas_call(
        _attn_kernel,
        grid_spec=pltpu.PrefetchScalarGridSpec(
            num_scalar_prefetch=3,
            grid=(b, n_steps),
            in_specs=[
                pl.BlockSpec((1, h, HEAD_SLOT, tq),
                             lambda bi, pr, qa, ka, fl: (bi, 0, 0, qa[at(bi, pr)])),
                pl.BlockSpec((1, h, tk, HEAD_SLOT),
                             lambda bi, pr, qa, ka, fl: (bi, 0, ka[at(bi, pr)], 0)),
                pl.BlockSpec((1, d, tk), lambda bi, pr, qa, ka, fl: (bi, 0, ka[at(bi, pr)])),
                pl.BlockSpec((1, tq, d), qtile),
                pl.BlockSpec((d, d), const2),
                pl.BlockSpec((1, d), const2),
                pl.BlockSpec((d, e), const2),
                pl.BlockSpec((1, e), const2),
                pl.BlockSpec((tr, tr), const2),
            ],
            out_specs=[
                pl.BlockSpec((1, tq, d), qtile),
                pl.BlockSpec((1, tq, d), qtile),
                pl.BlockSpec((tq, TOP_K), tok),
                pl.BlockSpec((tq, TOP_K), tok),
                pl.BlockSpec((n_sub, 1, e),
                             lambda bi, pr, qa, ka, fl: (bi * nq + qa[at(bi, pr)], 0, 0)),
            ],
            scratch_shapes=[
                pltpu.VMEM((h, 1, tq), F32),
                pltpu.VMEM((h, 1, tq), F32),
                pltpu.VMEM((d, tq), F32),
                pltpu.VMEM((2, tk, tq), F32),
            ],
        ),
        out_shape=[
            jax.ShapeDtypeStruct((b, s, d), F32),
            jax.ShapeDtypeStruct((b, s, d), BF16),
            jax.ShapeDtypeStruct((n, TOP_K), F32),
            jax.ShapeDtypeStruct((n, TOP_K), F32),
            jax.ShapeDtypeStruct((n // tr, 1, e), jnp.int32),
        ],
        compiler_params=_cparams(2),
        name="attn",
    )(qi_arr, kj_arr, flag_arr, qt, kp, vt, x3, w_o.astype(BF16), g_ffn.reshape(1, d),
      router_w, router_b.reshape(1, e), _lower_tri(tr, strict=True))


def _tiles(b, s):
    n = b * s
    tr = min(256, s)
    tm = min(512, n)
    tq = min(512, s)
    return tr, tm, tq


def _moe(h2, slots, cnt_tiles, w_gu, b_gu, w_down, b_down, layer, tr, tm):
    n, d = h2.shape
    e = w_gu.shape[1]
    num_tiles = _local_rows(tr, e) * (n // tr) // tm + e
    plan = _plan(cnt_tiles, tm, num_tiles)
    xs = _dispatch(h2, slots, plan, num_tiles * tm, tr, tm)
    ys = _experts(xs, plan, w_gu, b_gu[layer], w_down, b_down[layer], layer, tm)
    return plan, ys


def kernel(x, p, norm_mix, norm_ffn, norm_ple, norm_final, pool_w, pool_scale, kv_norm, w_k, w_v,
           w_fgate, b_fgate, w_q, w_o, router_w, router_b, exp_w_gu, exp_b_gu, exp_w_down,
           exp_b_down, ple_w_gate, ple_b_gate, ple_w_proj):
    b, s, d = x.shape
    n = b * s
    tr, tm, tq = _tiles(b, s)

    x1, h2, gates, slots, cnt_tiles = _mix0(
        x, norm_mix[0], pool_w[0], pool_scale[0], norm_ffn[0], router_w[0], router_b[0], tr)
    plan, ys = _moe(h2.reshape(n, d), slots, cnt_tiles,
                    exp_w_gu, exp_b_gu, exp_w_down, exp_b_down, 0, tr, tm)
    x3, qt, kp, vt, krow, qcol = _combine(
        plan, ys, x1, gates, slots, p, 0, norm_ple[0], ple_w_gate[0], ple_b_gate[0],
        ple_w_proj[0], tr, kv=(kv_norm, w_k, w_v, w_fgate, b_fgate, norm_mix[1], w_q[0]))

    x4, h2, gates, slots, cnt_tiles = _attn(
        qt, kp, vt, krow, qcol, x3, w_o[0], norm_ffn[1], router_w[1], router_b[1], tq, tr)
    plan, ys = _moe(h2.reshape(n, d), slots, cnt_tiles,
                    exp_w_gu, exp_b_gu, exp_w_down, exp_b_down, 1, tr, tm)
    return _combine(plan, ys, x4, gates, slots, p, 1, norm_ple[1], ple_w_gate[1], ple_b_gate[1],
                    ple_w_proj[1], tr, g_final=norm_final)
```

```python
import functools
import math

import numpy as np
import jax
import jax.numpy as jnp
from jax import lax
from jax.experimental import pallas as pl
from jax.experimental.pallas import tpu as pltpu

TOP_K = 4
POOL_WINDOWS = (2, 4, 8, 16)
POOL_HALO = 16
SWIGLU_LIMIT = 7.0
SWIGLU_ALPHA = 1.702
RMS_EPS = 1e-6
NEG_INF = -1e30
LOG2E = math.log2(math.e)
HEAD_SLOT = 128
N_DECAY_PIECES = 3
SUBLANES = 8
LONG_RUN_BIT = 3
N_KEY_STATS = 3
SKIP_LOG2_MARGIN = 160.0
NORM_SLACK = 1.05
STEP_ACTIVE, STEP_FIRST = 1, 2
F32 = jnp.float32
BF16 = jnp.bfloat16
NT_DIMS = (((1,), (1,)), ((), ()))

V7X_VMEM_BYTES = 64 * 1024 * 1024
VMEM_LIMIT = V7X_VMEM_BYTES - 8 * 1024 * 1024


def _cparams(n_axes):
    return pltpu.CompilerParams(
        dimension_semantics=("arbitrary",) * n_axes,
        vmem_limit_bytes=VMEM_LIMIT,
    )


def _rms(v, g):
    return v * lax.rsqrt(jnp.mean(v * v, axis=-1, keepdims=True) + RMS_EPS) * g


def _split3(a):
    hi = a.astype(BF16)
    r1 = a - hi.astype(F32)
    mid = r1.astype(BF16)
    lo = (r1 - mid.astype(F32)).astype(BF16)
    return hi, mid, lo


def _mm(a, b, dims):
    if dims is None:
        return jnp.dot(a, b, preferred_element_type=F32)
    return lax.dot_general(a, b, dims, preferred_element_type=F32)


def _dot_x3(a, b, dims=None):
    a_hi = a.astype(BF16)
    a_lo = (a - a_hi.astype(F32)).astype(BF16)
    b_hi = b.astype(BF16)
    b_lo = (b - b_hi.astype(F32)).astype(BF16)
    return _mm(a_hi, b_hi, dims) + _mm(a_hi, b_lo, dims) + _mm(a_lo, b_hi, dims)


def _dot_sel(sel, x, dims=None, sel_first=True):
    sel = sel.astype(BF16)
    out = None
    for piece in _split3(x):
        term = _mm(sel, piece, dims) if sel_first else _mm(piece, sel, dims)
        out = term if out is None else out + term
    return out


def _lower_tri(t, strict):
    r = np.arange(t)
    keep = r[None, :] < r[:, None] if strict else r[None, :] <= r[:, None]
    return jnp.asarray(keep, BF16)


def _route(h2, rw_ref, rb_ref, tri_ref, gate_ref, slot_ref, cnt_ref):
    t = h2.shape[0]
    e = rw_ref.shape[1]
    logits = _dot_x3(h2, rw_ref[...]) + rb_ref[...]
    lane = lax.broadcasted_iota(jnp.int32, (t, e), 1).astype(F32)
    work = logits
    sels, vals = [], []
    for _ in range(TOP_K):
        m = jnp.max(work, axis=-1, keepdims=True)
        idx = jnp.min(jnp.where(work == m, lane, float(e)), axis=-1, keepdims=True)
        sel = lane == idx
        work = jnp.where(sel, -jnp.inf, work)
        sels.append(sel)
        vals.append(m)
    exps = [jnp.exp(v - vals[0]) for v in vals]
    denom = exps[0] + exps[1] + exps[2] + exps[3]
    onehot = sels[0].astype(F32) + sels[1].astype(F32) + sels[2].astype(F32) + sels[3].astype(F32)
    excl = jnp.dot(tri_ref[...], onehot.astype(BF16), preferred_element_type=F32)
    cnt = jnp.sum(onehot, axis=0, keepdims=True)
    erow = lax.broadcasted_iota(jnp.int32, (e, e), 0)
    ecol = lax.broadcasted_iota(jnp.int32, (e, e), 1)
    run8 = jnp.floor((cnt + (SUBLANES - 1.0)) * (1.0 / SUBLANES))
    off = SUBLANES * jnp.dot(jnp.broadcast_to(run8, (SUBLANES, e)).astype(BF16),
                             (erow < ecol).astype(BF16), preferred_element_type=F32)[0:1, :]
    tot = excl + off
    lane_k = lax.broadcasted_iota(jnp.int32, (t, TOP_K), 1)
    gate_out = jnp.zeros((t, TOP_K), F32)
    slot_out = jnp.zeros((t, TOP_K), F32)
    for k in range(TOP_K):
        slot_k = jnp.sum(jnp.where(sels[k], tot, 0.0), axis=-1, keepdims=True)
        gate_out = jnp.where(lane_k == k, exps[k] / denom, gate_out)
        slot_out = jnp.where(lane_k == k, slot_k, slot_out)
    gate_ref[...] = gate_out
    slot_ref[...] = slot_out
    cnt_ref[...] = cnt.astype(jnp.int32)


def _mix0_kernel(x_ref, xh_ref, gmix_ref, pw_ref, ps_ref, gffn_ref, rw_ref, rb_ref, tri_ref,
                 x1_ref, h2_ref, gate_ref, slot_ref, cnt_ref):
    i = pl.program_id(1)
    ts = x_ref.shape[1]
    d = x_ref.shape[2]
    cg = d // len(POOL_WINDOWS)

    xt = x_ref[0]
    g = gmix_ref[...]
    h = _rms(xt, g)
    hh = jnp.where(i > 0, _rms(xh_ref[0], g), 0.0)
    hcat = jnp.concatenate([hh, h], axis=0)
    pos = i * ts + lax.broadcasted_iota(jnp.int32, (ts, 1), 0)
    outs = []
    for gi, w in enumerate(POOL_WINDOWS):
        cur = hcat[:, gi * cg:(gi + 1) * cg]
        span = 1
        while span < w:
            cur = cur + pltpu.roll(cur, span, 0)
            span *= 2
        wsum = cur[POOL_HALO:, :]
        count = jnp.minimum(pos + 1, w).astype(F32)
        diff = wsum / count - h[:, gi * cg:(gi + 1) * cg]
        outs.append(jnp.dot(diff.astype(BF16), pw_ref[gi], preferred_element_type=F32))
    mixed = jnp.concatenate(outs, axis=-1) * ps_ref[...]
    x1 = xt + mixed
    x1_ref[0] = x1
    h2 = _rms(x1, gffn_ref[...])
    h2_ref[0] = h2.astype(BF16)
    _route(h2, rw_ref, rb_ref, tri_ref, gate_ref, slot_ref, cnt_ref.at[0])


def _mix0(x, g_mix, pool_w, pool_scale, g_ffn, router_w, router_b, ts):
    b, s, d = x.shape
    e = router_w.shape[1]
    n = b * s
    nt = s // ts
    hb = ts // POOL_HALO
    ng = len(POOL_WINDOWS)
    cg = d // ng
    const2 = lambda bi, i: (0, 0)
    tok = lambda bi, i: (bi * nt + i, 0)
    return pl.pallas_call(
        _mix0_kernel,
        grid=(b, nt),
        in_specs=[
            pl.BlockSpec((1, ts, d), lambda bi, i: (bi, i, 0)),
            pl.BlockSpec((1, POOL_HALO, d), lambda bi, i: (bi, jnp.maximum(i * hb - 1, 0), 0)),
            pl.BlockSpec((1, d), const2),
            pl.BlockSpec((ng, cg, cg), lambda bi, i: (0, 0, 0)),
            pl.BlockSpec((1, d), const2),
            pl.BlockSpec((1, d), const2),
            pl.BlockSpec((d, e), const2),
            pl.BlockSpec((1, e), const2),
            pl.BlockSpec((ts, ts), const2),
        ],
        out_specs=[
            pl.BlockSpec((1, ts, d), lambda bi, i: (bi, i, 0)),
            pl.BlockSpec((1, ts, d), lambda bi, i: (bi, i, 0)),
            pl.BlockSpec((ts, TOP_K), tok),
            pl.BlockSpec((ts, TOP_K), tok),
            pl.BlockSpec((1, 1, e), lambda bi, i: (bi * nt + i, 0, 0)),
        ],
        out_shape=[
            jax.ShapeDtypeStruct((b, s, d), F32),
            jax.ShapeDtypeStruct((b, s, d), BF16),
            jax.ShapeDtypeStruct((n, TOP_K), F32),
            jax.ShapeDtypeStruct((n, TOP_K), F32),
            jax.ShapeDtypeStruct((n // ts, 1, e), jnp.int32),
        ],
        compiler_params=_cparams(2),
        name="mix0",
    )(x, x, g_mix.reshape(1, d), pool_w.astype(BF16), pool_scale.reshape(1, d),
      g_ffn.reshape(1, d), router_w, router_b.reshape(1, e), _lower_tri(ts, strict=True))


def _plan(cnt_tiles, tm, num_tiles):
    nt, _, e = cnt_tiles.shape
    tm8 = tm // SUBLANES
    run8 = (cnt_tiles.reshape(nt, e) + SUBLANES - 1) // SUBLANES
    total8 = jnp.sum(run8, axis=0)
    tiles_per = (total8 + tm8 - 1) // tm8
    tile_end = jnp.cumsum(tiles_per)
    tile_start = tile_end - tiles_per
    base = tile_start[None, :] * tm + SUBLANES * (jnp.cumsum(run8, axis=0) - run8)
    off = SUBLANES * (jnp.cumsum(run8, axis=1) - run8)
    pad_start = tile_start * tm + SUBLANES * total8
    pad8 = tiles_per * tm8 - total8
    n_used = tile_end[-1]
    t = jnp.arange(num_tiles, dtype=jnp.int32)
    tile_expert = jnp.sum((t[:, None] >= tile_end[None, :]).astype(jnp.int32), axis=1)
    tile_expert = jnp.minimum(tile_expert, e - 1)
    pad_total8 = jnp.sum(pad8)
    zero_full = pad_total8 // tm8 + (num_tiles - n_used)
    misc = jnp.stack([n_used, zero_full, pad_total8 % tm8])
    i32 = lambda a: a.reshape(-1).astype(jnp.int32)
    return dict(run8=i32(run8), base=i32(base), off=i32(off), tot8=i32(jnp.sum(run8, axis=1)),
                pad_start=i32(pad_start), pad8=i32(pad8), misc=i32(misc),
                tile_expert=i32(tile_expert), n_used=i32(n_used))


def _copy_run(n8, src, src_row, dst, dst_row, sem, max8):
    def copy_bit(bit):
        size = SUBLANES << bit

        @pl.when((n8 & (1 << bit)) != 0)
        def _():
            done = ((n8 >> (bit + 1)) << (bit + 1)) * SUBLANES
            pltpu.make_async_copy(
                src.at[pl.ds(pl.multiple_of(src_row + done, SUBLANES), size), :],
                dst.at[pl.ds(pl.multiple_of(dst_row + done, SUBLANES), size), :], sem).start()

    bits = list(reversed(range(max8.bit_length())))
    long_bits = [bit for bit in bits if bit >= LONG_RUN_BIT]
    if long_bits:
        @pl.when(n8 >= (1 << LONG_RUN_BIT))
        def _():
            for bit in long_bits:
                copy_bit(bit)
    for bit in bits:
        if bit < LONG_RUN_BIT:
            copy_bit(bit)


def _wait_run(n8, src, dst, sem, max8):
    for bit in reversed(range(max8.bit_length())):
        size = SUBLANES << bit

        @pl.when((n8 & (1 << bit)) != 0)
        def _():
            pltpu.make_async_copy(src.at[pl.ds(0, size), :], dst.at[pl.ds(0, size), :], sem).wait()


def _dispatch_kernel(run8_s, base_s, off_s, tot8_s, pstart_s, pad8_s, misc_s,
                     h_ref, slot_ref, eye_ref, xs_ref, xbuf, zbuf, sem, zsem):
    t = pl.program_id(0)
    nt = pl.num_programs(0)
    td = h_ref.shape[0]
    rows = xbuf.shape[1]
    tm = zbuf.shape[0]
    n_exp = pad8_s.shape[0]
    num_tiles = xs_ref.shape[0] // tm
    cur = t % 2

    def wait_step(step, s):
        _wait_run(tot8_s[step], xbuf.at[s], xs_ref, sem.at[s], rows // SUBLANES)

    @pl.when(t >= 2)
    def _():
        wait_step(t - 2, cur)

    srow = _dot_sel(eye_ref[...], slot_ref[...], NT_DIMS)
    j = lax.broadcasted_iota(jnp.int32, (rows, td), 0).astype(F32)
    pick = j == srow[0:1, :]
    for k in range(1, TOP_K):
        pick = pick | (j == srow[k:k + 1, :])
    xbuf[cur] = jnp.dot(pick.astype(BF16), h_ref[...], preferred_element_type=F32)

    for ex in range(n_exp):
        q = t * n_exp + ex
        _copy_run(run8_s[q], xbuf.at[cur], off_s[q], xs_ref, base_s[q], sem.at[cur],
                  td // SUBLANES)

    @pl.when(t == nt - 1)
    def _():
        zbuf[...] = jnp.zeros_like(zbuf)
        tm8 = tm // SUBLANES
        for ex in range(n_exp):
            _copy_run(pad8_s[ex], zbuf, 0, xs_ref, pstart_s[ex], zsem, tm8 - 1)

        def zero_tile(tile, c):
            pltpu.make_async_copy(zbuf, xs_ref.at[pl.ds(pl.multiple_of(tile * tm, tm), tm), :],
                                  zsem).start()
            return c

        lax.fori_loop(misc_s[0], num_tiles, zero_tile, 0)

        def wait_tile(_, c):
            pltpu.make_async_copy(zbuf, xs_ref.at[pl.ds(0, tm), :], zsem).wait()
            return c

        lax.fori_loop(0, misc_s[1], wait_tile, 0)
        _wait_run(misc_s[2], zbuf, xs_ref, zsem, tm8 - 1)
        wait_step(t, cur)

        @pl.when(nt >= 2)
        def _():
            wait_step(t - 1, 1 - cur)


def _local_rows(t, n_exp):
    return t * TOP_K + SUBLANES * n_exp


def _dispatch(h2, slots, plan, rows_padded, td, tm):
    n, d = h2.shape
    n_exp = plan["pad8"].shape[0]
    return pl.pallas_call(
        _dispatch_kernel,
        grid_spec=pltpu.PrefetchScalarGridSpec(
            num_scalar_prefetch=7,
            grid=(n // td,),
            in_specs=[
                pl.BlockSpec((td, d), lambda t, *_: (t, 0)),
                pl.BlockSpec((td, TOP_K), lambda t, *_: (t, 0)),
                pl.BlockSpec((SUBLANES, TOP_K), lambda t, *_: (0, 0)),
            ],
            out_specs=pl.BlockSpec(memory_space=pl.ANY),
            scratch_shapes=[
                pltpu.VMEM((2, _local_rows(td, n_exp), d), F32),
                pltpu.VMEM((tm, d), F32),
                pltpu.SemaphoreType.DMA((2,)),
                pltpu.SemaphoreType.DMA(()),
            ],
        ),
        out_shape=jax.ShapeDtypeStruct((rows_padded, d), F32),
        compiler_params=_cparams(1),
        name="dispatch",
    )(plan["run8"], plan["base"], plan["off"], plan["tot8"], plan["pad_start"], plan["pad8"],
      plan["misc"], h2, slots, jnp.eye(SUBLANES, TOP_K, dtype=F32))


def _experts_kernel(texp_ref, nused_ref, xs_ref, wgu_ref, bgu_ref, wd_ref, bd_ref, ys_ref,
                    wgu_bf, wd_bf):
    t = pl.program_id(0)
    f = wd_ref.shape[2]

    @pl.when((t == 0) | (texp_ref[t] != texp_ref[jnp.maximum(t - 1, 0)]))
    def _():
        wgu_bf[...] = wgu_ref[0, 0].astype(BF16)
        wd_bf[...] = wd_ref[0, 0].astype(BF16)

    @pl.when(t < nused_ref[0])
    def _():
        xb = xs_ref[...].astype(BF16)
        gu = jnp.dot(xb, wgu_bf[...], preferred_element_type=F32) + bgu_ref[0]
        gate = jnp.minimum(gu[:, :f], SWIGLU_LIMIT)
        up = jnp.clip(gu[:, f:], -SWIGLU_LIMIT, SWIGLU_LIMIT)
        y = (up + 1.0) * (gate * jax.nn.sigmoid(SWIGLU_ALPHA * gate))
        ys_ref[...] = jnp.dot(y.astype(BF16), wd_bf[...], preferred_element_type=F32) + bd_ref[0]

    @pl.when(t >= nused_ref[0])
    def _():
        ys_ref[...] = jnp.zeros_like(ys_ref)


def _experts(xs, plan, w_gu, b_gu, w_down, b_down, layer, tm):
    rows, d = xs.shape
    _, e, _, f2 = w_gu.shape
    f = f2 // 2
    num_tiles = rows // tm
    expert = lambda t, te, nu: (te[t], 0, 0)
    layer_expert = lambda t, te, nu: (layer, te[t], 0, 0)
    return pl.pallas_call(
        _experts_kernel,
        grid_spec=pltpu.PrefetchScalarGridSpec(
            num_scalar_prefetch=2,
            grid=(num_tiles,),
            in_specs=[
                pl.BlockSpec((tm, d),
                             lambda t, te, nu: (jnp.maximum(jnp.minimum(t, nu[0] - 1), 0), 0)),
                pl.BlockSpec((1, 1, d, f2), layer_expert),
                pl.BlockSpec((1, 1, f2), expert),
                pl.BlockSpec((1, 1, f, d), layer_expert),
                pl.BlockSpec((1, 1, d), expert),
            ],
            out_specs=pl.BlockSpec((tm, d), lambda t, te, nu: (t, 0)),
            scratch_shapes=[pltpu.VMEM((d, f2), BF16), pltpu.VMEM((f, d), BF16)],
        ),
        out_shape=jax.ShapeDtypeStruct((rows, d), F32),
        compiler_params=_cparams(1),
        name="experts",
    )(plan["tile_expert"], plan["n_used"], xs, w_gu, b_gu.reshape(e, 1, f2), w_down,
      b_down.reshape(e, 1, d))


def _combine_kernel(*refs, final, n_exp):
    if final:
        (run8_s, base_s, off_s, tot8_s, ys_hbm, x_ref, gates_ref, slot_ref, p_ref, gple_ref,
         wg_ref, bg_ref, wp_ref, gfin_ref, out_ref, ybuf, gsem) = refs
    else:
        (run8_s, base_s, off_s, tot8_s, ys_hbm, x_ref, gates_ref, slot_ref, p_ref, gple_ref,
         wg_ref, bg_ref, wp_ref,
         gkv_ref, wkp_ref, wvt_ref, wf_ref, bf_ref, gq_ref, wqt_ref,
         selk_ref, selqt_ref, onesk_ref, onesq_ref, eye_ref, tri_ref, hsum_ref, hsumt_ref,
         x3_ref, qt_ref, kp_ref, vt_ref, krow_ref, qcol_ref, ybuf, gsem, carry) = refs
    bi = pl.program_id(0)
    i = pl.program_id(1)
    nt = pl.num_programs(1)
    step = bi * nt + i
    nsteps = pl.num_programs(0) * nt
    tc = x_ref.shape[1]
    rows = ybuf.shape[1]
    cur = step % 2

    def fetch(s, slot):
        for ex in range(n_exp):
            q = s * n_exp + ex
            _copy_run(run8_s[q], ys_hbm, base_s[q], ybuf.at[slot], off_s[q], gsem.at[slot],
                      tc // SUBLANES)

    @pl.when(step == 0)
    def _():
        ybuf[...] = jnp.zeros_like(ybuf)
        fetch(step, cur)

    @pl.when(step + 1 < nsteps)
    def _():
        fetch(step + 1, 1 - cur)

    proj = jnp.dot(p_ref[0].astype(BF16), wp_ref[...], preferred_element_type=F32)
    x1 = x_ref[0]
    gates = gates_ref[...]
    slots = slot_ref[...]
    j = lax.broadcasted_iota(jnp.int32, (tc, rows), 1).astype(F32)
    gmat = jnp.where(j == slots[:, 0:1], gates[:, 0:1], 0.0)
    for k in range(1, TOP_K):
        gmat = gmat + jnp.where(j == slots[:, k:k + 1], gates[:, k:k + 1], 0.0)
    g_hi = gmat.astype(BF16)
    g_lo = (gmat - g_hi.astype(F32)).astype(BF16)

    _wait_run(tot8_s[step], ys_hbm, ybuf.at[cur], gsem.at[cur], rows // SUBLANES)
    yb = ybuf[cur].astype(BF16)
    moe = (jnp.dot(g_hi, yb, preferred_element_type=F32)
           + jnp.dot(g_lo, yb, preferred_element_type=F32))
    x2 = x1 + moe
    hn = _rms(x2, gple_ref[...])
    pg = jax.nn.sigmoid(jnp.dot(hn.astype(BF16), wg_ref[...], preferred_element_type=F32) + bg_ref[...])
    x3 = x2 + pg * proj

    if final:
        out_ref[0] = _rms(x3, gfin_ref[...])
        return

    x3_ref[0] = x3
    hk = _rms(x3, gkv_ref[...])
    hkb = hk.astype(BF16)
    n_heads = kp_ref.shape[1]
    vt_ref[0] = lax.dot_general(wvt_ref[...], hkb, NT_DIMS, preferred_element_type=F32).astype(BF16)

    z = _dot_x3(hk, wf_ref[...]) + bf_ref[...]
    lf = jax.nn.log_sigmoid(z) * LOG2E

    @pl.when(i == 0)
    def _():
        carry[...] = jnp.zeros_like(carry)

    a_c = _dot_sel(tri_ref[...], lf) + carry[...]
    carry[...] = a_c[tc - 1:tc, :]
    a_r = _dot_sel(eye_ref[...], a_c, NT_DIMS)

    def piece_of_copy(a, copy_index):
        out = None
        for jp, pc in enumerate(_split3(a)):
            out = pc if out is None else jnp.where(copy_index == jp, pc, out)
        return out

    pk = piece_of_copy(-a_c, lax.broadcasted_iota(jnp.int32, a_c.shape, 1) // n_heads)
    dh = wkp_ref.shape[1] // n_heads
    k_main = jnp.dot(hkb, wkp_ref[...], preferred_element_type=F32)
    k_aug = onesk_ref[...] + jnp.dot(pk, selk_ref[...], preferred_element_type=F32)
    hq = _rms(x3, gq_ref[...])
    pq = piece_of_copy(a_r, lax.broadcasted_iota(jnp.int32, a_r.shape, 0) // n_heads)
    q_main = lax.dot_general(wqt_ref[...], hq.astype(BF16), NT_DIMS, preferred_element_type=F32)
    q_aug = onesq_ref[...] + jnp.dot(selqt_ref[...], pq, preferred_element_type=F32)

    kb = k_main.astype(BF16).astype(F32)
    qb = q_main.astype(BF16).astype(F32)
    ksq = jnp.dot((kb * kb).astype(BF16), hsum_ref[...], preferred_element_type=F32)
    qsq = jnp.dot(hsumt_ref[...], (qb * qb).astype(BF16), preferred_element_type=F32)
    krow_ref[0, 0:1, :] = jnp.max(ksq, axis=0, keepdims=True)
    krow_ref[0, 1:2, :] = jnp.max(a_c[:, :n_heads], axis=0, keepdims=True)
    krow_ref[0, 2:3, :] = jnp.min(a_c[:, :n_heads], axis=0, keepdims=True)
    qcol_ref[0] = jnp.max(qsq, axis=1, keepdims=True)
    lane = lax.broadcasted_iota(jnp.int32, (tc, HEAD_SLOT), 1)
    for hd in range(n_heads):
        sl = slice(hd * HEAD_SLOT, (hd + 1) * HEAD_SLOT)
        pair = k_main[:, (hd // 2) * HEAD_SLOT:(hd // 2 + 1) * HEAD_SLOT]
        if hd % 2 == 1:
            pair = pltpu.roll(pair, dh, 1)
        kp_ref[0, hd] = jnp.where(lane < dh, pair, k_aug[:, sl]).astype(BF16)
        qt_ref[0, hd, 0:dh, :] = q_main[hd * dh:(hd + 1) * dh, :].astype(BF16)
        qt_ref[0, hd, dh:HEAD_SLOT, :] = q_aug[hd * HEAD_SLOT + dh:(hd + 1) * HEAD_SLOT, :].astype(BF16)


def _combine(plan, ys, x1, gates, slots, p_all, layer, g_ple, w_gate, b_gate, w_proj, tc, *,
             g_final=None, kv=None):
    b, s, d = x1.shape
    pd = p_all.shape[-1]
    p_rows = p_all.reshape(-1, s, pd)
    nt = s // tc
    final = g_final is not None
    const2 = lambda bi, i, *_: (0, 0)
    tile3 = lambda bi, i, *_: (bi, i, 0)
    tok = lambda bi, i, *_: (bi * nt + i, 0)
    in_specs = [
        pl.BlockSpec(memory_space=pl.ANY),
        pl.BlockSpec((1, tc, d), tile3),
        pl.BlockSpec((tc, TOP_K), tok),
        pl.BlockSpec((tc, TOP_K), tok),
        pl.BlockSpec((1, tc, pd), lambda bi, i, *_: (layer * b + bi, i, 0)),
        pl.BlockSpec((1, d), const2),
        pl.BlockSpec((d, d), const2),
        pl.BlockSpec((1, d), const2),
        pl.BlockSpec((pd, d), const2),
    ]
    args = [ys, x1, gates, slots, p_rows, g_ple.reshape(1, d), w_gate.astype(BF16),
            b_gate.reshape(1, d), w_proj.astype(BF16)]
    n_exp = plan["pad8"].shape[0]
    scratch = [
        pltpu.VMEM((2, _local_rows(tc, n_exp), d), F32),
        pltpu.SemaphoreType.DMA((2,)),
    ]
    if final:
        in_specs.append(pl.BlockSpec((1, d), const2))
        args.append(g_final.reshape(1, d))
        out_specs = pl.BlockSpec((1, tc, d), tile3)
        out_shape = jax.ShapeDtypeStruct((b, s, d), F32)
    else:
        g_kv, w_k, w_v, w_f, b_f, g_q, w_q = kv
        h = w_f.shape[1]
        dh = d // h
        hs = h * HEAD_SLOT
        npc = N_DECAY_PIECES

        assert HEAD_SLOT == 2 * dh, (HEAD_SLOT, dh)

        hc = npc * h
        selk = np.zeros((hc, hs), np.float32)
        selqt = np.zeros((hs, hc), np.float32)
        onesk = np.zeros((1, hs), np.float32)
        onesq = np.zeros((hs, 1), np.float32)
        hsum = np.zeros((d, h), np.float32)
        for hd in range(h):
            hsum[hd * dh:(hd + 1) * dh, hd] = 1.0
            for jp in range(npc):
                selqt[hd * HEAD_SLOT + dh + jp, jp * h + hd] = 1.0
                onesk[0, hd * HEAD_SLOT + dh + jp] = 1.0
                selk[jp * h + hd, hd * HEAD_SLOT + dh + npc + jp] = 1.0
                onesq[hd * HEAD_SLOT + dh + npc + jp, 0] = 1.0
        in_specs += [
            pl.BlockSpec((1, d), const2),
            pl.BlockSpec((d, d), const2),
            pl.BlockSpec((d, d), const2),
            pl.BlockSpec((d, hc), const2),
            pl.BlockSpec((1, hc), const2),
            pl.BlockSpec((1, d), const2),
            pl.BlockSpec((d, d), const2),
            pl.BlockSpec((hc, hs), const2),
            pl.BlockSpec((hs, hc), const2),
            pl.BlockSpec((1, hs), const2),
            pl.BlockSpec((hs, 1), const2),
            pl.BlockSpec((hc, hc), const2),
            pl.BlockSpec((tc, tc), const2),
            pl.BlockSpec((d, h), const2),
            pl.BlockSpec((h, d), const2),
        ]
        qscale = LOG2E / math.sqrt(dh)
        args += [g_kv.reshape(1, d), w_k.astype(BF16), w_v.T.astype(BF16),
                 jnp.tile(w_f, (1, npc)), jnp.tile(b_f, npc).reshape(1, hc), g_q.reshape(1, d),
                 (w_q * qscale).T.astype(BF16),
                 jnp.asarray(selk, BF16), jnp.asarray(selqt, BF16), jnp.asarray(onesk),
                 jnp.asarray(onesq), jnp.eye(hc, dtype=F32), _lower_tri(tc, strict=False),
                 jnp.asarray(hsum, BF16), jnp.asarray(hsum.T, BF16)]
        out_specs = [
            pl.BlockSpec((1, tc, d), tile3),
            pl.BlockSpec((1, h, HEAD_SLOT, tc), lambda bi, i, *_: (bi, 0, 0, i)),
            pl.BlockSpec((1, h, tc, HEAD_SLOT), lambda bi, i, *_: (bi, 0, i, 0)),
            pl.BlockSpec((1, d, tc), lambda bi, i, *_: (bi, 0, i)),
            pl.BlockSpec((1, N_KEY_STATS, h), lambda bi, i, *_: (bi * nt + i, 0, 0)),
            pl.BlockSpec((1, h, 1), lambda bi, i, *_: (bi * nt + i, 0, 0)),
        ]
        out_shape = [
            jax.ShapeDtypeStruct((b, s, d), F32),
            jax.ShapeDtypeStruct((b, h, HEAD_SLOT, s), BF16),
            jax.ShapeDtypeStruct((b, h, s, HEAD_SLOT), BF16),
            jax.ShapeDtypeStruct((b, d, s), BF16),
            jax.ShapeDtypeStruct((b * nt, N_KEY_STATS, h), F32),
            jax.ShapeDtypeStruct((b * nt, h, 1), F32),
        ]
        scratch += [pltpu.VMEM((1, hc), F32)]
    return pl.pallas_call(
        functools.partial(_combine_kernel, final=final, n_exp=n_exp),
        grid_spec=pltpu.PrefetchScalarGridSpec(
            num_scalar_prefetch=4,
            grid=(b, nt),
            in_specs=in_specs,
            out_specs=out_specs,
            scratch_shapes=scratch,
        ),
        out_shape=out_shape,
        compiler_params=_cparams(2),
        name="combine_final" if final else "combine_kv",
    )(plan["run8"], plan["base"], plan["off"], plan["tot8"], *args)


def _attn_kernel(qi_ref, kj_ref, flag_ref, qt_ref, kp_ref, vt_ref, x_ref, wo_ref,
                 gffn_ref, rw_ref, rb_ref, tri_ref,
                 x4_ref, h2_ref, gate_ref, slot_ref, cnt_ref,
                 m_ref, l_ref, acc_ref, st_ref):
    step = pl.program_id(0) * pl.num_programs(1) + pl.program_id(1)
    qi = qi_ref[step]
    kj = kj_ref[step]
    flags = flag_ref[step]
    active = (flags & STEP_ACTIVE) != 0
    n_heads = qt_ref.shape[1]
    tq = qt_ref.shape[3]
    tk = kp_ref.shape[2]
    dh = vt_ref.shape[1] // n_heads
    n_sub = cnt_ref.shape[0]
    tr = tq // n_sub

    @pl.when((flags & STEP_FIRST) != 0)
    def _():
        m_ref[...] = jnp.full_like(m_ref, NEG_INF)
        l_ref[...] = jnp.zeros_like(l_ref)
        acc_ref[...] = jnp.zeros_like(acc_ref)

    def logits(hd):
        st_ref[hd % 2] = jnp.dot(kp_ref[0, hd], qt_ref[0, hd], preferred_element_type=F32)

    def head_step(hd, masked):
        st = st_ref[hd % 2]
        if masked:
            krow = lax.broadcasted_iota(jnp.int32, (tk, tq), 0)
            qcol = lax.broadcasted_iota(jnp.int32, (tk, tq), 1)
            st = jnp.where(krow <= qcol, st, NEG_INF)
        m_old = m_ref[hd]
        m_new = jnp.maximum(m_old, jnp.max(st, axis=0, keepdims=True))
        a = jnp.exp2(m_old - m_new)
        pt = jnp.exp2(st - m_new)
        l_ref[hd] = a * l_ref[hd] + jnp.sum(pt, axis=0, keepdims=True)
        m_ref[hd] = m_new
        pv = jnp.dot(vt_ref[0, hd * dh:(hd + 1) * dh, :], pt.astype(BF16),
                     preferred_element_type=F32)
        acc_ref[hd * dh:(hd + 1) * dh, :] = a * acc_ref[hd * dh:(hd + 1) * dh, :] + pv

    def all_heads(masked):
        logits(0)
        for hd in range(n_heads):
            if hd + 1 < n_heads:
                logits(hd + 1)
            head_step(hd, masked)

    @pl.when(active & (kj < qi))
    def _():
        all_heads(False)

    @pl.when(active & (kj == qi))
    def _():
        all_heads(True)
        parts = [acc_ref[hd * dh:(hd + 1) * dh, :] * (1.0 / l_ref[hd]) for hd in range(n_heads)]
        o = jnp.concatenate(parts, axis=0).T.astype(BF16)
        x4 = x_ref[0] + jnp.dot(o, wo_ref[...], preferred_element_type=F32)
        x4_ref[0] = x4
        h2 = _rms(x4, gffn_ref[...])
        h2_ref[0] = h2.astype(BF16)
        for sb in range(n_sub):
            rs = slice(sb * tr, (sb + 1) * tr)
            _route(h2[rs], rw_ref, rb_ref, tri_ref, gate_ref.at[rs], slot_ref.at[rs],
                   cnt_ref.at[sb])


def _attn_plan(krow, qcol, b, nq, n_sub):
    h = krow.shape[-1]
    per_tile = lambda a: a.reshape(b, nq, n_sub, h)
    kk = jnp.max(per_tile(krow[:, 0, :]), axis=2)
    amax = jnp.max(per_tile(krow[:, 1, :]), axis=2)
    amin = jnp.min(per_tile(krow[:, 2, :]), axis=2)
    qq = jnp.max(per_tile(qcol[:, :, 0]), axis=2)
    upper = (NORM_SLACK * jnp.sqrt(qq[:, :, None, :] * kk[:, None, :, :])
             + amax[:, :, None, :] - amin[:, None, :, :])
    lower = -NORM_SLACK * jnp.sqrt(qq * kk)
    dead = jnp.all(upper <= lower[:, :, None, :] - SKIP_LOG2_MARGIN, axis=-1)

    pairs = [(i, j) for i in range(nq) for j in range(i + 1)]
    qi = np.array([p[0] for p in pairs], np.int32)
    kj = np.array([p[1] for p in pairs], np.int32)
    n_steps = len(pairs)
    keep = jnp.logical_or(jnp.asarray(qi == kj)[None, :], jnp.logical_not(dead[:, qi, kj]))
    order = jnp.argsort(jnp.logical_not(keep).astype(jnp.int32), axis=1, stable=True)
    count = jnp.sum(keep.astype(jnp.int32), axis=1, keepdims=True)
    pos = jnp.arange(n_steps, dtype=jnp.int32)[None, :]
    active = pos < count
    clamp = jnp.minimum(pos, count - 1)
    qi_c = jnp.take_along_axis(jnp.asarray(qi)[order], clamp, axis=1)
    kj_c = jnp.take_along_axis(jnp.asarray(kj)[order], clamp, axis=1)
    prev_qi = jnp.concatenate([jnp.full((b, 1), -1, jnp.int32), qi_c[:, :-1]], axis=1)
    first = jnp.logical_and(active, qi_c != prev_qi)
    flags = active.astype(jnp.int32) * STEP_ACTIVE + first.astype(jnp.int32) * STEP_FIRST
    i32 = lambda a: a.reshape(-1).astype(jnp.int32)
    return i32(qi_c), i32(kj_c), i32(flags), n_steps


def _attn(qt, kp, vt, krow, qcol, x3, w_o, g_ffn, router_w, router_b, tq, tr):
    b, s, d = x3.shape
    h = qt.shape[1]
    e = router_w.shape[1]
    n = b * s
    nq = s // tq
    tk = tq
    n_sub = tq // tr
    qi_arr, kj_arr, flag_arr, n_steps = _attn_plan(krow, qcol, b, nq, n_sub)
    at = lambda bi, pr: bi * n_steps + pr
    qtile = lambda bi, pr, qa, ka, fl: (bi, qa[at(bi, pr)], 0)
    const2 = lambda bi, pr, qa, ka, fl: (0, 0)
    tok = lambda bi, pr, qa, ka, fl: (bi * nq + qa[at(bi, pr)], 0)
    return pl.pallas_call(
        _attn_kernel,
        grid_spec=pltpu.PrefetchScalarGridSpec(
            num_scalar_prefetch=3,
            grid=(b, n_steps),
            in_specs=[
                pl.BlockSpec((1, h, HEAD_SLOT, tq),
                             lambda bi, pr, qa, ka, fl: (bi, 0, 0, qa[at(bi, pr)])),
                pl.BlockSpec((1, h, tk, HEAD_SLOT),
                             lambda bi, pr, qa, ka, fl: (bi, 0, ka[at(bi, pr)], 0)),
                pl.BlockSpec((1, d, tk), lambda bi, pr, qa, ka, fl: (bi, 0, ka[at(bi, pr)])),
                pl.BlockSpec((1, tq, d), qtile),
                pl.BlockSpec((d, d), const2),
                pl.BlockSpec((1, d), const2),
                pl.BlockSpec((d, e), const2),
                pl.BlockSpec((1, e), const2),
                pl.BlockSpec((tr, tr), const2),
            ],
            out_specs=[
                pl.BlockSpec((1, tq, d), qtile),
                pl.BlockSpec((1, tq, d), qtile),
                pl.BlockSpec((tq, TOP_K), tok),
                pl.BlockSpec((tq, TOP_K), tok),
                pl.BlockSpec((n_sub, 1, e),
                             lambda bi, pr, qa, ka, fl: (bi * nq + qa[at(bi, pr)], 0, 0)),
            ],
            scratch_shapes=[
                pltpu.VMEM((h, 1, tq), F32),
                pltpu.VMEM((h, 1, tq), F32),
                pltpu.VMEM((d, tq), F32),
                pltpu.VMEM((2, tk, tq), F32),
            ],
        ),
        out_shape=[
            jax.ShapeDtypeStruct((b, s, d), F32),
            jax.ShapeDtypeStruct((b, s, d), BF16),
            jax.ShapeDtypeStruct((n, TOP_K), F32),
            jax.ShapeDtypeStruct((n, TOP_K), F32),
            jax.ShapeDtypeStruct((n // tr, 1, e), jnp.int32),
        ],
        compiler_params=_cparams(2),
        name="attn",
    )(qi_arr, kj_arr, flag_arr, qt, kp, vt, x3, w_o.astype(BF16), g_ffn.reshape(1, d),
      router_w, router_b.reshape(1, e), _lower_tri(tr, strict=True))


def _tiles(b, s):
    n = b * s
    tr = min(256, s)
    tm = min(512, n)
    tq = min(512, s)
    return tr, tm, tq


def _moe(h2, slots, cnt_tiles, w_gu, b_gu, w_down, b_down, layer, tr, tm):
    n, d = h2.shape
    e = w_gu.shape[1]
    num_tiles = _local_rows(tr, e) * (n // tr) // tm + e
    plan = _plan(cnt_tiles, tm, num_tiles)
    xs = _dispatch(h2, slots, plan, num_tiles * tm, tr, tm)
    ys = _experts(xs, plan, w_gu, b_gu[layer], w_down, b_down[layer], layer, tm)
    return plan, ys


def kernel(x, p, norm_mix, norm_ffn, norm_ple, norm_final, pool_w, pool_scale, kv_norm, w_k, w_v,
           w_fgate, b_fgate, w_q, w_o, router_w, router_b, exp_w_gu, exp_b_gu, exp_w_down,
           exp_b_down, ple_w_gate, ple_b_gate, ple_w_proj):
    b, s, d = x.shape
    n = b * s
    tr, tm, tq = _tiles(b, s)

    x1, h2, gates, slots, cnt_tiles = _mix0(
        x, norm_mix[0], pool_w[0], pool_scale[0], norm_ffn[0], router_w[0], router_b[0], tr)
    plan, ys = _moe(h2.reshape(n, d), slots, cnt_tiles,
                    exp_w_gu, exp_b_gu, exp_w_down, exp_b_down, 0, tr, tm)
    x3, qt, kp, vt, krow, qcol = _combine(
        plan, ys, x1, gates, slots, p, 0, norm_ple[0], ple_w_gate[0], ple_b_gate[0],
        ple_w_proj[0], tr, kv=(kv_norm, w_k, w_v, w_fgate, b_fgate, norm_mix[1], w_q[0]))

    x4, h2, gates, slots, cnt_tiles = _attn(
        qt, kp, vt, krow, qcol, x3, w_o[0], norm_ffn[1], router_w[1], router_b[1], tq, tr)
    plan, ys = _moe(h2.reshape(n, d), slots, cnt_tiles,
                    exp_w_gu, exp_b_gu, exp_w_down, exp_b_down, 1, tr, tm)
    return _combine(plan, ys, x4, gates, slots, p, 1, norm_ple[1], ple_w_gate[1], ple_b_gate[1],
                    ple_w_proj[1], tr, g_final=norm_final)
```

```python
import functools
import math

import numpy as np
import jax
import jax.numpy as jnp
from jax import lax
from jax.experimental import pallas as pl
from jax.experimental.pallas import tpu as pltpu

TOP_K = 4
POOL_WINDOWS = (2, 4, 8, 16)
POOL_HALO = 16
SWIGLU_LIMIT = 7.0
SWIGLU_ALPHA = 1.702
RMS_EPS = 1e-6
NEG_INF = -1e30
LOG2E = math.log2(math.e)
HEAD_SLOT = 128
N_DECAY_PIECES = 3
SUBLANES = 8
LONG_RUN_BIT = 3
N_KEY_STATS = 3
SKIP_LOG2_MARGIN = 160.0
NORM_SLACK = 1.05
STEP_ACTIVE, STEP_FIRST = 1, 2
LOGIT_BUFFERS = 3
F32 = jnp.float32
BF16 = jnp.bfloat16
NT_DIMS = (((1,), (1,)), ((), ()))

V7X_VMEM_BYTES = 64 * 1024 * 1024
VMEM_LIMIT = V7X_VMEM_BYTES - 8 * 1024 * 1024


def _cparams(n_axes):
    return pltpu.CompilerParams(
        dimension_semantics=("arbitrary",) * n_axes,
        vmem_limit_bytes=VMEM_LIMIT,
    )


def _rms(v, g):
    return v * lax.rsqrt(jnp.mean(v * v, axis=-1, keepdims=True) + RMS_EPS) * g


def _split3(a):
    hi = a.astype(BF16)
    r1 = a - hi.astype(F32)
    mid = r1.astype(BF16)
    lo = (r1 - mid.astype(F32)).astype(BF16)
    return hi, mid, lo


def _mm(a, b, dims):
    if dims is None:
        return jnp.dot(a, b, preferred_element_type=F32)
    return lax.dot_general(a, b, dims, preferred_element_type=F32)


def _dot_x3(a, b, dims=None):
    a_hi = a.astype(BF16)
    a_lo = (a - a_hi.astype(F32)).astype(BF16)
    b_hi = b.astype(BF16)
    b_lo = (b - b_hi.astype(F32)).astype(BF16)
    return _mm(a_hi, b_hi, dims) + _mm(a_hi, b_lo, dims) + _mm(a_lo, b_hi, dims)


def _dot_sel(sel, x, dims=None, sel_first=True):
    sel = sel.astype(BF16)
    out = None
    for piece in _split3(x):
        term = _mm(sel, piece, dims) if sel_first else _mm(piece, sel, dims)
        out = term if out is None else out + term
    return out


def _lower_tri(t, strict):
    r = np.arange(t)
    keep = r[None, :] < r[:, None] if strict else r[None, :] <= r[:, None]
    return jnp.asarray(keep, BF16)


def _route(h2, rw_ref, rb_ref, tri_ref, gate_ref, slot_ref, cnt_ref):
    t = h2.shape[0]
    e = rw_ref.shape[1]
    logits = _dot_x3(h2, rw_ref[...]) + rb_ref[...]
    lane = lax.broadcasted_iota(jnp.int32, (t, e), 1).astype(F32)
    work = logits
    sels, vals = [], []
    for _ in range(TOP_K):
        m = jnp.max(work, axis=-1, keepdims=True)
        idx = jnp.min(jnp.where(work == m, lane, float(e)), axis=-1, keepdims=True)
        sel = lane == idx
        work = jnp.where(sel, -jnp.inf, work)
        sels.append(sel)
        vals.append(m)
    exps = [jnp.exp(v - vals[0]) for v in vals]
    denom = exps[0] + exps[1] + exps[2] + exps[3]
    onehot = sels[0].astype(F32) + sels[1].astype(F32) + sels[2].astype(F32) + sels[3].astype(F32)
    excl = jnp.dot(tri_ref[...], onehot.astype(BF16), preferred_element_type=F32)
    cnt = jnp.sum(onehot, axis=0, keepdims=True)
    erow = lax.broadcasted_iota(jnp.int32, (e, e), 0)
    ecol = lax.broadcasted_iota(jnp.int32, (e, e), 1)
    run8 = jnp.floor((cnt + (SUBLANES - 1.0)) * (1.0 / SUBLANES))
    off = SUBLANES * jnp.dot(jnp.broadcast_to(run8, (SUBLANES, e)).astype(BF16),
                             (erow < ecol).astype(BF16), preferred_element_type=F32)[0:1, :]
    tot = excl + off
    lane_k = lax.broadcasted_iota(jnp.int32, (t, TOP_K), 1)
    gate_out = jnp.zeros((t, TOP_K), F32)
    slot_out = jnp.zeros((t, TOP_K), F32)
    for k in range(TOP_K):
        slot_k = jnp.sum(jnp.where(sels[k], tot, 0.0), axis=-1, keepdims=True)
        gate_out = jnp.where(lane_k == k, exps[k] / denom, gate_out)
        slot_out = jnp.where(lane_k == k, slot_k, slot_out)
    gate_ref[...] = gate_out
    slot_ref[...] = slot_out
    cnt_ref[...] = cnt.astype(jnp.int32)


def _mix0_kernel(x_ref, xh_ref, gmix_ref, pw_ref, ps_ref, gffn_ref, rw_ref, rb_ref, tri_ref,
                 x1_ref, h2_ref, gate_ref, slot_ref, cnt_ref):
    i = pl.program_id(1)
    ts = x_ref.shape[1]
    d = x_ref.shape[2]
    cg = d // len(POOL_WINDOWS)

    xt = x_ref[0]
    g = gmix_ref[...]
    h = _rms(xt, g)
    hh = jnp.where(i > 0, _rms(xh_ref[0], g), 0.0)
    hcat = jnp.concatenate([hh, h], axis=0)
    pos = i * ts + lax.broadcasted_iota(jnp.int32, (ts, 1), 0)
    outs = []
    for gi, w in enumerate(POOL_WINDOWS):
        cur = hcat[:, gi * cg:(gi + 1) * cg]
        span = 1
        while span < w:
            cur = cur + pltpu.roll(cur, span, 0)
            span *= 2
        wsum = cur[POOL_HALO:, :]
        count = jnp.minimum(pos + 1, w).astype(F32)
        diff = wsum / count - h[:, gi * cg:(gi + 1) * cg]
        outs.append(jnp.dot(diff.astype(BF16), pw_ref[gi], preferred_element_type=F32))
    mixed = jnp.concatenate(outs, axis=-1) * ps_ref[...]
    x1 = xt + mixed
    x1_ref[0] = x1
    h2 = _rms(x1, gffn_ref[...])
    h2_ref[0] = h2.astype(BF16)
    _route(h2, rw_ref, rb_ref, tri_ref, gate_ref, slot_ref, cnt_ref.at[0])


def _mix0(x, g_mix, pool_w, pool_scale, g_ffn, router_w, router_b, ts):
    b, s, d = x.shape
    e = router_w.shape[1]
    n = b * s
    nt = s // ts
    hb = ts // POOL_HALO
    ng = len(POOL_WINDOWS)
    cg = d // ng
    const2 = lambda bi, i: (0, 0)
    tok = lambda bi, i: (bi * nt + i, 0)
    return pl.pallas_call(
        _mix0_kernel,
        grid=(b, nt),
        in_specs=[
            pl.BlockSpec((1, ts, d), lambda bi, i: (bi, i, 0)),
            pl.BlockSpec((1, POOL_HALO, d), lambda bi, i: (bi, jnp.maximum(i * hb - 1, 0), 0)),
            pl.BlockSpec((1, d), const2),
            pl.BlockSpec((ng, cg, cg), lambda bi, i: (0, 0, 0)),
            pl.BlockSpec((1, d), const2),
            pl.BlockSpec((1, d), const2),
            pl.BlockSpec((d, e), const2),
            pl.BlockSpec((1, e), const2),
            pl.BlockSpec((ts, ts), const2),
        ],
        out_specs=[
            pl.BlockSpec((1, ts, d), lambda bi, i: (bi, i, 0)),
            pl.BlockSpec((1, ts, d), lambda bi, i: (bi, i, 0)),
            pl.BlockSpec((ts, TOP_K), tok),
            pl.BlockSpec((ts, TOP_K), tok),
            pl.BlockSpec((1, 1, e), lambda bi, i: (bi * nt + i, 0, 0)),
        ],
        out_shape=[
            jax.ShapeDtypeStruct((b, s, d), F32),
            jax.ShapeDtypeStruct((b, s, d), BF16),
            jax.ShapeDtypeStruct((n, TOP_K), F32),
            jax.ShapeDtypeStruct((n, TOP_K), F32),
            jax.ShapeDtypeStruct((n // ts, 1, e), jnp.int32),
        ],
        compiler_params=_cparams(2),
        name="mix0",
    )(x, x, g_mix.reshape(1, d), pool_w.astype(BF16), pool_scale.reshape(1, d),
      g_ffn.reshape(1, d), router_w, router_b.reshape(1, e), _lower_tri(ts, strict=True))


def _plan(cnt_tiles, tm, num_tiles):
    nt, _, e = cnt_tiles.shape
    tm8 = tm // SUBLANES
    run8 = (cnt_tiles.reshape(nt, e) + SUBLANES - 1) // SUBLANES
    total8 = jnp.sum(run8, axis=0)
    tiles_per = (total8 + tm8 - 1) // tm8
    tile_end = jnp.cumsum(tiles_per)
    tile_start = tile_end - tiles_per
    base = tile_start[None, :] * tm + SUBLANES * (jnp.cumsum(run8, axis=0) - run8)
    off = SUBLANES * (jnp.cumsum(run8, axis=1) - run8)
    pad_start = tile_start * tm + SUBLANES * total8
    pad8 = tiles_per * tm8 - total8
    n_used = tile_end[-1]
    t = jnp.arange(num_tiles, dtype=jnp.int32)
    tile_expert = jnp.sum((t[:, None] >= tile_end[None, :]).astype(jnp.int32), axis=1)
    tile_expert = jnp.minimum(tile_expert, e - 1)
    pad_total8 = jnp.sum(pad8)
    zero_full = pad_total8 // tm8 + (num_tiles - n_used)
    misc = jnp.stack([n_used, zero_full, pad_total8 % tm8])
    i32 = lambda a: a.reshape(-1).astype(jnp.int32)
    return dict(run8=i32(run8), base=i32(base), off=i32(off), tot8=i32(jnp.sum(run8, axis=1)),
                pad_start=i32(pad_start), pad8=i32(pad8), misc=i32(misc),
                tile_expert=i32(tile_expert), n_used=i32(n_used))


def _copy_run(n8, src, src_row, dst, dst_row, sem, max8):
    def copy_bit(bit):
        size = SUBLANES << bit

        @pl.when((n8 & (1 << bit)) != 0)
        def _():
            done = ((n8 >> (bit + 1)) << (bit + 1)) * SUBLANES
            pltpu.make_async_copy(
                src.at[pl.ds(pl.multiple_of(src_row + done, SUBLANES), size), :],
                dst.at[pl.ds(pl.multiple_of(dst_row + done, SUBLANES), size), :], sem).start()

    bits = list(reversed(range(max8.bit_length())))
    long_bits = [bit for bit in bits if bit >= LONG_RUN_BIT]
    if long_bits:
        @pl.when(n8 >= (1 << LONG_RUN_BIT))
        def _():
            for bit in long_bits:
                copy_bit(bit)
    for bit in bits:
        if bit < LONG_RUN_BIT:
            copy_bit(bit)


def _wait_run(n8, src, dst, sem, max8):
    for bit in reversed(range(max8.bit_length())):
        size = SUBLANES << bit

        @pl.when((n8 & (1 << bit)) != 0)
        def _():
            pltpu.make_async_copy(src.at[pl.ds(0, size), :], dst.at[pl.ds(0, size), :], sem).wait()


def _dispatch_kernel(run8_s, base_s, off_s, tot8_s, pstart_s, pad8_s, misc_s,
                     h_ref, slot_ref, eye_ref, xs_ref, xbuf, zbuf, sem, zsem):
    t = pl.program_id(0)
    nt = pl.num_programs(0)
    td = h_ref.shape[0]
    rows = xbuf.shape[1]
    tm = zbuf.shape[0]
    n_exp = pad8_s.shape[0]
    num_tiles = xs_ref.shape[0] // tm
    cur = t % 2

    def wait_step(step, s):
        _wait_run(tot8_s[step], xbuf.at[s], xs_ref, sem.at[s], rows // SUBLANES)

    @pl.when(t >= 2)
    def _():
        wait_step(t - 2, cur)

    srow = _dot_sel(eye_ref[...], slot_ref[...], NT_DIMS)
    j = lax.broadcasted_iota(jnp.int32, (rows, td), 0).astype(F32)
    pick = j == srow[0:1, :]
    for k in range(1, TOP_K):
        pick = pick | (j == srow[k:k + 1, :])
    xbuf[cur] = jnp.dot(pick.astype(BF16), h_ref[...], preferred_element_type=F32)

    for ex in range(n_exp):
        q = t * n_exp + ex
        _copy_run(run8_s[q], xbuf.at[cur], off_s[q], xs_ref, base_s[q], sem.at[cur],
                  td // SUBLANES)

    @pl.when(t == nt - 1)
    def _():
        zbuf[...] = jnp.zeros_like(zbuf)
        tm8 = tm // SUBLANES
        for ex in range(n_exp):
            _copy_run(pad8_s[ex], zbuf, 0, xs_ref, pstart_s[ex], zsem, tm8 - 1)

        def zero_tile(tile, c):
            pltpu.make_async_copy(zbuf, xs_ref.at[pl.ds(pl.multiple_of(tile * tm, tm), tm), :],
                                  zsem).start()
            return c

        lax.fori_loop(misc_s[0], num_tiles, zero_tile, 0)

        def wait_tile(_, c):
            pltpu.make_async_copy(zbuf, xs_ref.at[pl.ds(0, tm), :], zsem).wait()
            return c

        lax.fori_loop(0, misc_s[1], wait_tile, 0)
        _wait_run(misc_s[2], zbuf, xs_ref, zsem, tm8 - 1)
        wait_step(t, cur)

        @pl.when(nt >= 2)
        def _():
            wait_step(t - 1, 1 - cur)


def _local_rows(t, n_exp):
    return t * TOP_K + SUBLANES * n_exp


def _dispatch(h2, slots, plan, rows_padded, td, tm):
    n, d = h2.shape
    n_exp = plan["pad8"].shape[0]
    return pl.pallas_call(
        _dispatch_kernel,
        grid_spec=pltpu.PrefetchScalarGridSpec(
            num_scalar_prefetch=7,
            grid=(n // td,),
            in_specs=[
                pl.BlockSpec((td, d), lambda t, *_: (t, 0)),
                pl.BlockSpec((td, TOP_K), lambda t, *_: (t, 0)),
                pl.BlockSpec((SUBLANES, TOP_K), lambda t, *_: (0, 0)),
            ],
            out_specs=pl.BlockSpec(memory_space=pl.ANY),
            scratch_shapes=[
                pltpu.VMEM((2, _local_rows(td, n_exp), d), F32),
                pltpu.VMEM((tm, d), F32),
                pltpu.SemaphoreType.DMA((2,)),
                pltpu.SemaphoreType.DMA(()),
            ],
        ),
        out_shape=jax.ShapeDtypeStruct((rows_padded, d), F32),
        compiler_params=_cparams(1),
        name="dispatch",
    )(plan["run8"], plan["base"], plan["off"], plan["tot8"], plan["pad_start"], plan["pad8"],
      plan["misc"], h2, slots, jnp.eye(SUBLANES, TOP_K, dtype=F32))


def _experts_kernel(texp_ref, nused_ref, xs_ref, wgu_ref, bgu_ref, wd_ref, bd_ref, ys_ref,
                    wgu_bf, wd_bf):
    t = pl.program_id(0)
    f = wd_ref.shape[2]

    @pl.when((t == 0) | (texp_ref[t] != texp_ref[jnp.maximum(t - 1, 0)]))
    def _():
        wgu_bf[...] = wgu_ref[0, 0].astype(BF16)
        wd_bf[...] = wd_ref[0, 0].astype(BF16)

    @pl.when(t < nused_ref[0])
    def _():
        xb = xs_ref[...].astype(BF16)
        gu = jnp.dot(xb, wgu_bf[...], preferred_element_type=F32) + bgu_ref[0]
        gate = jnp.minimum(gu[:, :f], SWIGLU_LIMIT)
        up = jnp.clip(gu[:, f:], -SWIGLU_LIMIT, SWIGLU_LIMIT)
        y = (up + 1.0) * (gate * jax.nn.sigmoid(SWIGLU_ALPHA * gate))
        ys_ref[...] = jnp.dot(y.astype(BF16), wd_bf[...], preferred_element_type=F32) + bd_ref[0]

    @pl.when(t >= nused_ref[0])
    def _():
        ys_ref[...] = jnp.zeros_like(ys_ref)


def _experts(xs, plan, w_gu, b_gu, w_down, b_down, layer, tm):
    rows, d = xs.shape
    _, e, _, f2 = w_gu.shape
    f = f2 // 2
    num_tiles = rows // tm
    expert = lambda t, te, nu: (te[t], 0, 0)
    layer_expert = lambda t, te, nu: (layer, te[t], 0, 0)
    return pl.pallas_call(
        _experts_kernel,
        grid_spec=pltpu.PrefetchScalarGridSpec(
            num_scalar_prefetch=2,
            grid=(num_tiles,),
            in_specs=[
                pl.BlockSpec((tm, d),
                             lambda t, te, nu: (jnp.maximum(jnp.minimum(t, nu[0] - 1), 0), 0)),
                pl.BlockSpec((1, 1, d, f2), layer_expert),
                pl.BlockSpec((1, 1, f2), expert),
                pl.BlockSpec((1, 1, f, d), layer_expert),
                pl.BlockSpec((1, 1, d), expert),
            ],
            out_specs=pl.BlockSpec((tm, d), lambda t, te, nu: (t, 0)),
            scratch_shapes=[pltpu.VMEM((d, f2), BF16), pltpu.VMEM((f, d), BF16)],
        ),
        out_shape=jax.ShapeDtypeStruct((rows, d), F32),
        compiler_params=_cparams(1),
        name="experts",
    )(plan["tile_expert"], plan["n_used"], xs, w_gu, b_gu.reshape(e, 1, f2), w_down,
      b_down.reshape(e, 1, d))


def _combine_kernel(*refs, final, n_exp):
    if final:
        (run8_s, base_s, off_s, tot8_s, ys_hbm, x_ref, gates_ref, slot_ref, p_ref, gple_ref,
         wg_ref, bg_ref, wp_ref, gfin_ref, out_ref, ybuf, gsem) = refs
    else:
        (run8_s, base_s, off_s, tot8_s, ys_hbm, x_ref, gates_ref, slot_ref, p_ref, gple_ref,
         wg_ref, bg_ref, wp_ref,
         gkv_ref, wkp_ref, wvt_ref, wf_ref, bf_ref, gq_ref, wqt_ref,
         selk_ref, selqt_ref, onesk_ref, onesq_ref, eye_ref, tri_ref, hsum_ref, hsumt_ref,
         x3_ref, qt_ref, kp_ref, vt_ref, krow_ref, qcol_ref, ybuf, gsem, carry) = refs
    bi = pl.program_id(0)
    i = pl.program_id(1)
    nt = pl.num_programs(1)
    step = bi * nt + i
    nsteps = pl.num_programs(0) * nt
    tc = x_ref.shape[1]
    rows = ybuf.shape[1]
    cur = step % 2

    def fetch(s, slot):
        for ex in range(n_exp):
            q = s * n_exp + ex
            _copy_run(run8_s[q], ys_hbm, base_s[q], ybuf.at[slot], off_s[q], gsem.at[slot],
                      tc // SUBLANES)

    @pl.when(step == 0)
    def _():
        ybuf[...] = jnp.zeros_like(ybuf)
        fetch(step, cur)

    @pl.when(step + 1 < nsteps)
    def _():
        fetch(step + 1, 1 - cur)

    proj = jnp.dot(p_ref[0].astype(BF16), wp_ref[...], preferred_element_type=F32)
    x1 = x_ref[0]
    gates = gates_ref[...]
    slots = slot_ref[...]
    j = lax.broadcasted_iota(jnp.int32, (tc, rows), 1).astype(F32)
    gmat = jnp.where(j == slots[:, 0:1], gates[:, 0:1], 0.0)
    for k in range(1, TOP_K):
        gmat = gmat + jnp.where(j == slots[:, k:k + 1], gates[:, k:k + 1], 0.0)
    g_hi = gmat.astype(BF16)
    g_lo = (gmat - g_hi.astype(F32)).astype(BF16)

    _wait_run(tot8_s[step], ys_hbm, ybuf.at[cur], gsem.at[cur], rows // SUBLANES)
    yb = ybuf[cur].astype(BF16)
    moe = (jnp.dot(g_hi, yb, preferred_element_type=F32)
           + jnp.dot(g_lo, yb, preferred_element_type=F32))
    x2 = x1 + moe
    hn = _rms(x2, gple_ref[...])
    pg = jax.nn.sigmoid(jnp.dot(hn.astype(BF16), wg_ref[...], preferred_element_type=F32) + bg_ref[...])
    x3 = x2 + pg * proj

    if final:
        out_ref[0] = _rms(x3, gfin_ref[...])
        return

    x3_ref[0] = x3
    hk = _rms(x3, gkv_ref[...])
    hkb = hk.astype(BF16)
    n_heads = kp_ref.shape[1]
    vt_ref[0] = lax.dot_general(wvt_ref[...], hkb, NT_DIMS, preferred_element_type=F32).astype(BF16)

    z = _dot_x3(hk, wf_ref[...]) + bf_ref[...]
    lf = jax.nn.log_sigmoid(z) * LOG2E

    @pl.when(i == 0)
    def _():
        carry[...] = jnp.zeros_like(carry)

    a_c = _dot_sel(tri_ref[...], lf) + carry[...]
    carry[...] = a_c[tc - 1:tc, :]
    a_r = _dot_sel(eye_ref[...], a_c, NT_DIMS)

    def piece_of_copy(a, copy_index):
        out = None
        for jp, pc in enumerate(_split3(a)):
            out = pc if out is None else jnp.where(copy_index == jp, pc, out)
        return out

    pk = piece_of_copy(-a_c, lax.broadcasted_iota(jnp.int32, a_c.shape, 1) // n_heads)
    dh = wkp_ref.shape[1] // n_heads
    k_main = jnp.dot(hkb, wkp_ref[...], preferred_element_type=F32)
    k_aug = onesk_ref[...] + jnp.dot(pk, selk_ref[...], preferred_element_type=F32)
    hq = _rms(x3, gq_ref[...])
    pq = piece_of_copy(a_r, lax.broadcasted_iota(jnp.int32, a_r.shape, 0) // n_heads)
    q_main = lax.dot_general(wqt_ref[...], hq.astype(BF16), NT_DIMS, preferred_element_type=F32)
    q_aug = onesq_ref[...] + jnp.dot(selqt_ref[...], pq, preferred_element_type=F32)

    kb = k_main.astype(BF16).astype(F32)
    qb = q_main.astype(BF16).astype(F32)
    ksq = jnp.dot((kb * kb).astype(BF16), hsum_ref[...], preferred_element_type=F32)
    qsq = jnp.dot(hsumt_ref[...], (qb * qb).astype(BF16), preferred_element_type=F32)
    krow_ref[0, 0:1, :] = jnp.max(ksq, axis=0, keepdims=True)
    krow_ref[0, 1:2, :] = jnp.max(a_c[:, :n_heads], axis=0, keepdims=True)
    krow_ref[0, 2:3, :] = jnp.min(a_c[:, :n_heads], axis=0, keepdims=True)
    qcol_ref[0] = jnp.max(qsq, axis=1, keepdims=True)
    lane = lax.broadcasted_iota(jnp.int32, (tc, HEAD_SLOT), 1)
    for hd in range(n_heads):
        sl = slice(hd * HEAD_SLOT, (hd + 1) * HEAD_SLOT)
        pair = k_main[:, (hd // 2) * HEAD_SLOT:(hd // 2 + 1) * HEAD_SLOT]
        if hd % 2 == 1:
            pair = pltpu.roll(pair, dh, 1)
        kp_ref[0, hd] = jnp.where(lane < dh, pair, k_aug[:, sl]).astype(BF16)
        qt_ref[0, hd, 0:dh, :] = q_main[hd * dh:(hd + 1) * dh, :].astype(BF16)
        qt_ref[0, hd, dh:HEAD_SLOT, :] = q_aug[hd * HEAD_SLOT + dh:(hd + 1) * HEAD_SLOT, :].astype(BF16)


def _combine(plan, ys, x1, gates, slots, p_all, layer, g_ple, w_gate, b_gate, w_proj, tc, *,
             g_final=None, kv=None):
    b, s, d = x1.shape
    pd = p_all.shape[-1]
    p_rows = p_all.reshape(-1, s, pd)
    nt = s // tc
    final = g_final is not None
    const2 = lambda bi, i, *_: (0, 0)
    tile3 = lambda bi, i, *_: (bi, i, 0)
    tok = lambda bi, i, *_: (bi * nt + i, 0)
    in_specs = [
        pl.BlockSpec(memory_space=pl.ANY),
        pl.BlockSpec((1, tc, d), tile3),
        pl.BlockSpec((tc, TOP_K), tok),
        pl.BlockSpec((tc, TOP_K), tok),
        pl.BlockSpec((1, tc, pd), lambda bi, i, *_: (layer * b + bi, i, 0)),
        pl.BlockSpec((1, d), const2),
        pl.BlockSpec((d, d), const2),
        pl.BlockSpec((1, d), const2),
        pl.BlockSpec((pd, d), const2),
    ]
    args = [ys, x1, gates, slots, p_rows, g_ple.reshape(1, d), w_gate.astype(BF16),
            b_gate.reshape(1, d), w_proj.astype(BF16)]
    n_exp = plan["pad8"].shape[0]
    scratch = [
        pltpu.VMEM((2, _local_rows(tc, n_exp), d), F32),
        pltpu.SemaphoreType.DMA((2,)),
    ]
    if final:
        in_specs.append(pl.BlockSpec((1, d), const2))
        args.append(g_final.reshape(1, d))
        out_specs = pl.BlockSpec((1, tc, d), tile3)
        out_shape = jax.ShapeDtypeStruct((b, s, d), F32)
    else:
        g_kv, w_k, w_v, w_f, b_f, g_q, w_q = kv
        h = w_f.shape[1]
        dh = d // h
        hs = h * HEAD_SLOT
        npc = N_DECAY_PIECES

        assert HEAD_SLOT == 2 * dh, (HEAD_SLOT, dh)

        hc = npc * h
        selk = np.zeros((hc, hs), np.float32)
        selqt = np.zeros((hs, hc), np.float32)
        onesk = np.zeros((1, hs), np.float32)
        onesq = np.zeros((hs, 1), np.float32)
        hsum = np.zeros((d, h), np.float32)
        for hd in range(h):
            hsum[hd * dh:(hd + 1) * dh, hd] = 1.0
            for jp in range(npc):
                selqt[hd * HEAD_SLOT + dh + jp, jp * h + hd] = 1.0
                onesk[0, hd * HEAD_SLOT + dh + jp] = 1.0
                selk[jp * h + hd, hd * HEAD_SLOT + dh + npc + jp] = 1.0
                onesq[hd * HEAD_SLOT + dh + npc + jp, 0] = 1.0
        in_specs += [
            pl.BlockSpec((1, d), const2),
            pl.BlockSpec((d, d), const2),
            pl.BlockSpec((d, d), const2),
            pl.BlockSpec((d, hc), const2),
            pl.BlockSpec((1, hc), const2),
            pl.BlockSpec((1, d), const2),
            pl.BlockSpec((d, d), const2),
            pl.BlockSpec((hc, hs), const2),
            pl.BlockSpec((hs, hc), const2),
            pl.BlockSpec((1, hs), const2),
            pl.BlockSpec((hs, 1), const2),
            pl.BlockSpec((hc, hc), const2),
            pl.BlockSpec((tc, tc), const2),
            pl.BlockSpec((d, h), const2),
            pl.BlockSpec((h, d), const2),
        ]
        qscale = LOG2E / math.sqrt(dh)
        args += [g_kv.reshape(1, d), w_k.astype(BF16), w_v.T.astype(BF16),
                 jnp.tile(w_f, (1, npc)), jnp.tile(b_f, npc).reshape(1, hc), g_q.reshape(1, d),
                 (w_q * qscale).T.astype(BF16),
                 jnp.asarray(selk, BF16), jnp.asarray(selqt, BF16), jnp.asarray(onesk),
                 jnp.asarray(onesq), jnp.eye(hc, dtype=F32), _lower_tri(tc, strict=False),
                 jnp.asarray(hsum, BF16), jnp.asarray(hsum.T, BF16)]
        out_specs = [
            pl.BlockSpec((1, tc, d), tile3),
            pl.BlockSpec((1, h, HEAD_SLOT, tc), lambda bi, i, *_: (bi, 0, 0, i)),
            pl.BlockSpec((1, h, tc, HEAD_SLOT), lambda bi, i, *_: (bi, 0, i, 0)),
            pl.BlockSpec((1, d, tc), lambda bi, i, *_: (bi, 0, i)),
            pl.BlockSpec((1, N_KEY_STATS, h), lambda bi, i, *_: (bi * nt + i, 0, 0)),
            pl.BlockSpec((1, h, 1), lambda bi, i, *_: (bi * nt + i, 0, 0)),
        ]
        out_shape = [
            jax.ShapeDtypeStruct((b, s, d), F32),
            jax.ShapeDtypeStruct((b, h, HEAD_SLOT, s), BF16),
            jax.ShapeDtypeStruct((b, h, s, HEAD_SLOT), BF16),
            jax.ShapeDtypeStruct((b, d, s), BF16),
            jax.ShapeDtypeStruct((b * nt, N_KEY_STATS, h), F32),
            jax.ShapeDtypeStruct((b * nt, h, 1), F32),
        ]
        scratch += [pltpu.VMEM((1, hc), F32)]
    return pl.pallas_call(
        functools.partial(_combine_kernel, final=final, n_exp=n_exp),
        grid_spec=pltpu.PrefetchScalarGridSpec(
            num_scalar_prefetch=4,
            grid=(b, nt),
            in_specs=in_specs,
            out_specs=out_specs,
            scratch_shapes=scratch,
        ),
        out_shape=out_shape,
        compiler_params=_cparams(2),
        name="combine_final" if final else "combine_kv",
    )(plan["run8"], plan["base"], plan["off"], plan["tot8"], *args)


def _attn_kernel(qi_ref, kj_ref, flag_ref, qt_ref, kp_ref, vt_ref, x_ref, wo_ref,
                 gffn_ref, rw_ref, rb_ref, tri_ref,
                 x4_ref, h2_ref, gate_ref, slot_ref, cnt_ref,
                 m_ref, l_ref, acc_ref, st_ref):
    step = pl.program_id(0) * pl.num_programs(1) + pl.program_id(1)
    qi = qi_ref[step]
    kj = kj_ref[step]
    flags = flag_ref[step]
    active = (flags & STEP_ACTIVE) != 0
    n_heads = qt_ref.shape[1]
    tq = qt_ref.shape[3]
    tk = kp_ref.shape[2]
    dh = vt_ref.shape[1] // n_heads
    n_sub = cnt_ref.shape[0]
    tr = tq // n_sub

    @pl.when((flags & STEP_FIRST) != 0)
    def _():
        m_ref[...] = jnp.full_like(m_ref, NEG_INF)
        l_ref[...] = jnp.zeros_like(l_ref)
        acc_ref[...] = jnp.zeros_like(acc_ref)

    def logits(hd):
        st_ref[hd % LOGIT_BUFFERS] = jnp.dot(kp_ref[0, hd], qt_ref[0, hd],
                                             preferred_element_type=F32)

    def head_step(hd, masked):
        st = st_ref[hd % LOGIT_BUFFERS]
        if masked:
            krow = lax.broadcasted_iota(jnp.int32, (tk, tq), 0)
            qcol = lax.broadcasted_iota(jnp.int32, (tk, tq), 1)
            st = jnp.where(krow <= qcol, st, NEG_INF)
        m_old = m_ref[hd]
        m_new = jnp.maximum(m_old, jnp.max(st, axis=0, keepdims=True))
        a = jnp.exp2(m_old - m_new)
        pt = jnp.exp2(st - m_new)
        l_ref[hd] = a * l_ref[hd] + jnp.sum(pt, axis=0, keepdims=True)
        m_ref[hd] = m_new
        pv = jnp.dot(vt_ref[0, hd * dh:(hd + 1) * dh, :], pt.astype(BF16),
                     preferred_element_type=F32)
        acc_ref[hd * dh:(hd + 1) * dh, :] = a * acc_ref[hd * dh:(hd + 1) * dh, :] + pv

    def all_heads(masked):
        ahead = LOGIT_BUFFERS - 1
        for hd in range(min(ahead, n_heads)):
            logits(hd)
        for hd in range(n_heads):
            if hd + ahead < n_heads:
                logits(hd + ahead)
            head_step(hd, masked)

    @pl.when(active & (kj < qi))
    def _():
        all_heads(False)

    @pl.when(active & (kj == qi))
    def _():
        all_heads(True)
        parts = [acc_ref[hd * dh:(hd + 1) * dh, :] * (1.0 / l_ref[hd]) for hd in range(n_heads)]
        o = jnp.concatenate(parts, axis=0).T.astype(BF16)
        x4 = x_ref[0] + jnp.dot(o, wo_ref[...], preferred_element_type=F32)
        x4_ref[0] = x4
        h2 = _rms(x4, gffn_ref[...])
        h2_ref[0] = h2.astype(BF16)
        for sb in range(n_sub):
            rs = slice(sb * tr, (sb + 1) * tr)
            _route(h2[rs], rw_ref, rb_ref, tri_ref, gate_ref.at[rs], slot_ref.at[rs],
                   cnt_ref.at[sb])


def _attn_plan(krow, qcol, b, nq, n_sub):
    h = krow.shape[-1]
    per_tile = lambda a: a.reshape(b, nq, n_sub, h)
    kk = jnp.max(per_tile(krow[:, 0, :]), axis=2)
    amax = jnp.max(per_tile(krow[:, 1, :]), axis=2)
    amin = jnp.min(per_tile(krow[:, 2, :]), axis=2)
    qq = jnp.max(per_tile(qcol[:, :, 0]), axis=2)
    upper = (NORM_SLACK * jnp.sqrt(qq[:, :, None, :] * kk[:, None, :, :])
             + amax[:, :, None, :] - amin[:, None, :, :])
    lower = -NORM_SLACK * jnp.sqrt(qq * kk)
    dead = jnp.all(upper <= lower[:, :, None, :] - SKIP_LOG2_MARGIN, axis=-1)

    pairs = [(i, j) for i in range(nq) for j in range(i + 1)]
    qi = np.array([p[0] for p in pairs], np.int32)
    kj = np.array([p[1] for p in pairs], np.int32)
    n_steps = len(pairs)
    keep = jnp.logical_or(jnp.asarray(qi == kj)[None, :], jnp.logical_not(dead[:, qi, kj]))
    order = jnp.argsort(jnp.logical_not(keep).astype(jnp.int32), axis=1, stable=True)
    count = jnp.sum(keep.astype(jnp.int32), axis=1, keepdims=True)
    pos = jnp.arange(n_steps, dtype=jnp.int32)[None, :]
    active = pos < count
    clamp = jnp.minimum(pos, count - 1)
    qi_c = jnp.take_along_axis(jnp.asarray(qi)[order], clamp, axis=1)
    kj_c = jnp.take_along_axis(jnp.asarray(kj)[order], clamp, axis=1)
    prev_qi = jnp.concatenate([jnp.full((b, 1), -1, jnp.int32), qi_c[:, :-1]], axis=1)
    first = jnp.logical_and(active, qi_c != prev_qi)
    flags = active.astype(jnp.int32) * STEP_ACTIVE + first.astype(jnp.int32) * STEP_FIRST
    i32 = lambda a: a.reshape(-1).astype(jnp.int32)
    return i32(qi_c), i32(kj_c), i32(flags), n_steps


def _attn(qt, kp, vt, krow, qcol, x3, w_o, g_ffn, router_w, router_b, tq, tr):
    b, s, d = x3.shape
    h = qt.shape[1]
    e = router_w.shape[1]
    n = b * s
    nq = s // tq
    tk = tq
    n_sub = tq // tr
    qi_arr, kj_arr, flag_arr, n_steps = _attn_plan(krow, qcol, b, nq, n_sub)
    at = lambda bi, pr: bi * n_steps + pr
    qtile = lambda bi, pr, qa, ka, fl: (bi, qa[at(bi, pr)], 0)
    const2 = lambda bi, pr, qa, ka, fl: (0, 0)
    tok = lambda bi, pr, qa, ka, fl: (bi * nq + qa[at(bi, pr)], 0)
    return pl.pallas_call(
        _attn_kernel,
        grid_spec=pltpu.PrefetchScalarGridSpec(
            num_scalar_prefetch=3,
            grid=(b, n_steps),
            in_specs=[
                pl.BlockSpec((1, h, HEAD_SLOT, tq),
                             lambda bi, pr, qa, ka, fl: (bi, 0, 0, qa[at(bi, pr)])),
                pl.BlockSpec((1, h, tk, HEAD_SLOT),
                             lambda bi, pr, qa, ka, fl: (bi, 0, ka[at(bi, pr)], 0)),
                pl.BlockSpec((1, d, tk), lambda bi, pr, qa, ka, fl: (bi, 0, ka[at(bi, pr)])),
                pl.BlockSpec((1, tq, d), qtile),
                pl.BlockSpec((d, d), const2),
                pl.BlockSpec((1, d), const2),
                pl.BlockSpec((d, e), const2),
                pl.BlockSpec((1, e), const2),
                pl.BlockSpec((tr, tr), const2),
            ],
            out_specs=[
                pl.BlockSpec((1, tq, d), qtile),
                pl.BlockSpec((1, tq, d), qtile),
                pl.BlockSpec((tq, TOP_K), tok),
                pl.BlockSpec((tq, TOP_K), tok),
                pl.BlockSpec((n_sub, 1, e),
                             lambda bi, pr, qa, ka, fl: (bi * nq + qa[at(bi, pr)], 0, 0)),
            ],
            scratch_shapes=[
                pltpu.VMEM((h, 1, tq), F32),
                pltpu.VMEM((h, 1, tq), F32),
                pltpu.VMEM((d, tq), F32),
                pltpu.VMEM((LOGIT_BUFFERS, tk, tq), F32),
            ],
        ),
        out_shape=[
            jax.ShapeDtypeStruct((b, s, d), F32),
            jax.ShapeDtypeStruct((b, s, d), BF16),
            jax.ShapeDtypeStruct((n, TOP_K), F32),
            jax.ShapeDtypeStruct((n, TOP_K), F32),
            jax.ShapeDtypeStruct((n // tr, 1, e), jnp.int32),
        ],
        compiler_params=_cparams(2),
        name="attn",
    )(qi_arr, kj_arr, flag_arr, qt, kp, vt, x3, w_o.astype(BF16), g_ffn.reshape(1, d),
      router_w, router_b.reshape(1, e), _lower_tri(tr, strict=True))


def _tiles(b, s):
    n = b * s
    tr = min(256, s)
    tm = min(512, n)
    tq = min(512, s)
    return tr, tm, tq


def _moe(h2, slots, cnt_tiles, w_gu, b_gu, w_down, b_down, layer, tr, tm):
    n, d = h2.shape
    e = w_gu.shape[1]
    num_tiles = _local_rows(tr, e) * (n // tr) // tm + e
    plan = _plan(cnt_tiles, tm, num_tiles)
    xs = _dispatch(h2, slots, plan, num_tiles * tm, tr, tm)
    ys = _experts(xs, plan, w_gu, b_gu[layer], w_down, b_down[layer], layer, tm)
    return plan, ys


def kernel(x, p, norm_mix, norm_ffn, norm_ple, norm_final, pool_w, pool_scale, kv_norm, w_k, w_v,
           w_fgate, b_fgate, w_q, w_o, router_w, router_b, exp_w_gu, exp_b_gu, exp_w_down,
           exp_b_down, ple_w_gate, ple_b_gate, ple_w_proj):
    b, s, d = x.shape
    n = b * s
    tr, tm, tq = _tiles(b, s)

    x1, h2, gates, slots, cnt_tiles = _mix0(
        x, norm_mix[0], pool_w[0], pool_scale[0], norm_ffn[0], router_w[0], router_b[0], tr)
    plan, ys = _moe(h2.reshape(n, d), slots, cnt_tiles,
                    exp_w_gu, exp_b_gu, exp_w_down, exp_b_down, 0, tr, tm)
    x3, qt, kp, vt, krow, qcol = _combine(
        plan, ys, x1, gates, slots, p, 0, norm_ple[0], ple_w_gate[0], ple_b_gate[0],
        ple_w_proj[0], tr, kv=(kv_norm, w_k, w_v, w_fgate, b_fgate, norm_mix[1], w_q[0]))

    x4, h2, gates, slots, cnt_tiles = _attn(
        qt, kp, vt, krow, qcol, x3, w_o[0], norm_ffn[1], router_w[1], router_b[1], tq, tr)
    plan, ys = _moe(h2.reshape(n, d), slots, cnt_tiles,
                    exp_w_gu, exp_b_gu, exp_w_down, exp_b_down, 1, tr, tm)
    return _combine(plan, ys, x4, gates, slots, p, 1, norm_ple[1], ple_w_gate[1], ple_b_gate[1],
                    ple_w_proj[1], tr, g_final=norm_final)
```
